```python
import math
import jax, jax.numpy as jnp
from jax import lax
import numpy as np

D_MODEL = 2048
BATCH = 4
SEQ = 2048
DEPTH = 4
DEC_BATCH = 128
DEC_SEQ = 1
PAST_LEN = 16384
PAGE_SIZE = 128

N_MIXERS = 2
N_RET = (DEPTH + 1) // 2
N_CONV = DEPTH // 2
RET_HEADS = 8
RET_DK = D_MODEL // RET_HEADS
RET_DV = 2 * D_MODEL // RET_HEADS
RET_CHUNK = 128
ROPE_BASE = 10000.0
CONV_WIDTH = 31
PEER_HEADS = 8
PEER_NKEYS = 128
PEER_EXPERTS = PEER_NKEYS * PEER_NKEYS
PEER_DQ = 256
PEER_TOPK = 16
PEER_BLOCK = 128
NORM_EPS = 1e-6

kernel_name = 'hybrid_retention_conformer_peer_step'


def rmsnorm(x, w):
    xf = x.astype(jnp.float32)
    y = xf * lax.rsqrt(jnp.mean(xf * xf, axis=-1, keepdims=True) + NORM_EPS)
    return (y * w.astype(jnp.float32)).astype(x.dtype)


def rope(x, pos):
    half = x.shape[-1] // 2
    inv = ROPE_BASE ** (-jnp.arange(half, dtype=jnp.float32) / half)
    ang = pos[:, None] * inv[None, :]
    cos = jnp.cos(ang)[None, :, None, :]
    sin = jnp.sin(ang)[None, :, None, :]
    x1, x2 = x[..., :half], x[..., half:]
    return jnp.concatenate([x1 * cos - x2 * sin, x1 * sin + x2 * cos], axis=-1)


def retention(h, r0, w_in, gn_w, w_out, pos):
    B, L, _ = h.shape
    f32 = jnp.float32
    proj = h @ w_in
    q, k, v, g = jnp.split(proj, [D_MODEL, 2 * D_MODEL, 4 * D_MODEL], axis=-1)
    q = rope(q.reshape(B, L, RET_HEADS, RET_DK).astype(f32), pos)
    k = rope(k.reshape(B, L, RET_HEADS, RET_DK).astype(f32), pos) * (RET_DK ** -0.5)
    v = v.reshape(B, L, RET_HEADS, RET_DV).astype(f32)
    log_gamma = jnp.log(1.0 - 2.0 ** (-5.0 - jnp.arange(RET_HEADS, dtype=f32)))
    C = math.gcd(L, RET_CHUNK)
    nC = L // C
    idx = jnp.arange(C, dtype=f32)
    diff = idx[:, None] - idx[None, :]
    dmask = jnp.where(diff[None] >= 0,
                      jnp.exp(jnp.maximum(diff, 0.0)[None] * log_gamma[:, None, None]), 0.0)
    xi = jnp.exp((idx + 1.0)[:, None] * log_gamma[None, :])
    zeta = jnp.exp((C - 1.0 - idx)[:, None] * log_gamma[None, :])
    chunk_decay = jnp.exp(C * log_gamma)

    def to_chunks(a):
        return jnp.moveaxis(a.reshape(B, nC, C, *a.shape[2:]), 1, 0)

    def step(r, qkv):
        qc, kc, vc = qkv
        sc = jnp.einsum('bnhk,bmhk->bhnm', qc, kc) * dmask[None]
        o = (jnp.einsum('bhnm,bmhv->bnhv', sc, vc)
             + jnp.einsum('bnhk,bhkv->bnhv', qc, r) * xi[None, :, :, None])
        r = (r * chunk_decay[None, :, None, None]
             + jnp.einsum('bmhk,bmhv->bhkv', kc * zeta[None, :, :, None], vc))
        return r, o

    r, o = lax.scan(step, r0.astype(f32), (to_chunks(q), to_chunks(k), to_chunks(v)))
    o = jnp.moveaxis(o, 0, 1).reshape(B, L, RET_HEADS, RET_DV)
    mu = jnp.mean(o, axis=-1, keepdims=True)
    var = jnp.mean(jnp.square(o - mu), axis=-1, keepdims=True)
    o = (o - mu) * lax.rsqrt(var + NORM_EPS)
    o = o.reshape(B, L, RET_HEADS * RET_DV) * gn_w.astype(f32) * jax.nn.silu(g.astype(f32))
    return o.astype(h.dtype) @ w_out, r


def conv_module(h, buf, w_pw1, b_pw1, w_dw, b_dw, ln_w, ln_b, w_pw2, b_pw2):
    a, b = jnp.split(h @ w_pw1 + b_pw1, 2, axis=-1)
    u = a * jax.nn.sigmoid(b)
    full = jnp.concatenate([buf.astype(u.dtype), u], axis=1)
    y = lax.conv_general_dilated(full, w_dw.astype(full.dtype)[:, None, :], window_strides=(1,),
                                 padding='VALID', dimension_numbers=('NWC', 'WIO', 'NWC'),
                                 feature_group_count=D_MODEL) + b_dw
    yf = y.astype(jnp.float32)
    mu = jnp.mean(yf, axis=-1, keepdims=True)
    var = jnp.mean(jnp.square(yf - mu), axis=-1, keepdims=True)
    yf = (yf - mu) * lax.rsqrt(var + NORM_EPS) * ln_w.astype(jnp.float32) + ln_b.astype(jnp.float32)
    out = jax.nn.silu(yf).astype(h.dtype) @ w_pw2 + b_pw2
    return out, full[:, -(CONV_WIDTH - 1):]


def peer(h, w_q, keys, u, v):
    B, L, D = h.shape
    T = B * L
    nb = -(-T // PEER_BLOCK)
    t = jnp.pad(h.reshape(T, D), ((0, nb * PEER_BLOCK - T), (0, 0))).reshape(nb, PEER_BLOCK, D)
    keys_f = keys.astype(jnp.float32)

    def block(tb):
        q = (tb @ w_q).astype(jnp.float32).reshape(PEER_BLOCK, PEER_HEADS, 2, PEER_DQ // 2)
        s = jnp.einsum('thpc,hpnc->thpn', q, keys_f)
        s1, i1 = lax.top_k(s[:, :, 0], PEER_TOPK)
        s2, i2 = lax.top_k(s[:, :, 1], PEER_TOPK)
        cand = (s1[..., :, None] + s2[..., None, :]).reshape(PEER_BLOCK, PEER_HEADS, PEER_TOPK * PEER_TOPK)
        top, ci = lax.top_k(cand, PEER_TOPK)
        e = (jnp.take_along_axis(i1, ci // PEER_TOPK, axis=-1) * PEER_NKEYS
             + jnp.take_along_axis(i2, ci % PEER_TOPK, axis=-1))
        gate = jax.nn.softmax(top, axis=-1)
        act = jax.nn.gelu(jnp.einsum('thkd,td->thk', u[e], tb).astype(jnp.float32), approximate=False)
        return jnp.einsum('thk,thkd->td', (gate * act).astype(tb.dtype), v[e])

    out = lax.map(block, t)
    return out.reshape(nb * PEER_BLOCK, D)[:T].reshape(B, L, D)


def trunk(x, c, ret_state, conv_state, pos, ada_w, ada_b, norm_w, final_norm_w,
          ret_w_in, ret_gn_w, ret_w_out, conv_w_pw1, conv_b_pw1, conv_w_dw, conv_b_dw,
          conv_ln_w, conv_ln_b, conv_w_pw2, conv_b_pw2, peer_w_q, peer_keys, peer_u, peer_v):
    new_ret, new_conv = [], []
    sc_c = jax.nn.silu(c)
    for i in range(DEPTH):
        mod = sc_c @ ada_w[i] + ada_b[i]
        sh1, s1, g1, sh2, s2, g2 = [m[:, None, :] for m in jnp.split(mod, 6, axis=-1)]
        h = rmsnorm(x, norm_w[i, 0]) * (1.0 + s1) + sh1
        j = i // N_MIXERS
        if i % N_MIXERS == 0:
            y, r = retention(h, ret_state[j], ret_w_in[j], ret_gn_w[j], ret_w_out[j], pos)
            new_ret.append(r.astype(ret_state.dtype))
        else:
            y, b = conv_module(h, conv_state[j], conv_w_pw1[j], conv_b_pw1[j], conv_w_dw[j], conv_b_dw[j],
                               conv_ln_w[j], conv_ln_b[j], conv_w_pw2[j], conv_b_pw2[j])
            new_conv.append(b.astype(conv_state.dtype))
        x = x + g1 * y
        h = rmsnorm(x, norm_w[i, 1]) * (1.0 + s2) + sh2
        x = x + g2 * peer(h, peer_w_q[i], peer_keys[i], peer_u[i], peer_v[i])
    return rmsnorm(x, final_norm_w), jnp.stack(new_ret), jnp.stack(new_conv)


def setup_inputs(seed: int = 0) -> dict:
    key = jax.random.key(seed)
    ks = jax.random.split(key, 26)
    D = D_MODEL

    def nrm(k, shape, s):
        return jax.random.normal(k, shape, jnp.float32) * s

    return {
        'x_prompt': nrm(ks[0], (BATCH, SEQ, D), 1.0),
        'x_sample': nrm(ks[1], (DEC_BATCH, DEC_SEQ, D), 1.0),
        'state_ret': nrm(ks[2], (N_RET, DEC_BATCH, RET_HEADS, RET_DK, RET_DV), RET_DK ** -0.5),
        'state_conv': nrm(ks[3], (N_CONV, DEC_BATCH, CONV_WIDTH - 1, D), 0.5),
        'c_prompt': nrm(ks[4], (BATCH, D), 1.0),
        'c_sample': nrm(ks[5], (DEC_BATCH, D), 1.0),
        'ada_w': nrm(ks[6], (DEPTH, D, 6 * D), 0.5 * D ** -0.5),
        'ada_b': nrm(ks[7], (DEPTH, 6 * D), 0.02),
        'norm_w': 1.0 + nrm(ks[8], (DEPTH, 2, D), 0.02),
        'final_norm_w': 1.0 + nrm(ks[9], (D,), 0.02),
        'ret_w_in': nrm(ks[10], (N_RET, D, 6 * D), D ** -0.5),
        'ret_gn_w': 1.0 + nrm(ks[11], (N_RET, RET_HEADS * RET_DV), 0.02),
        'ret_w_out': nrm(ks[12], (N_RET, 2 * D, D), (2 * D) ** -0.5),
        'conv_w_pw1': nrm(ks[13], (N_CONV, D, 2 * D), D ** -0.5),
        'conv_b_pw1': nrm(ks[14], (N_CONV, 2 * D), 0.02),
        'conv_w_dw': nrm(ks[15], (N_CONV, CONV_WIDTH, D), CONV_WIDTH ** -0.5),
        'conv_b_dw': nrm(ks[16], (N_CONV, D), 0.02),
        'conv_ln_w': 1.0 + nrm(ks[17], (N_CONV, D), 0.02),
        'conv_ln_b': nrm(ks[18], (N_CONV, D), 0.02),
        'conv_w_pw2': nrm(ks[19], (N_CONV, D, D), D ** -0.5),
        'conv_b_pw2': nrm(ks[20], (N_CONV, D), 0.02),
        'peer_w_q': nrm(ks[21], (DEPTH, D, PEER_HEADS * PEER_DQ), D ** -0.5),
        'peer_keys': nrm(ks[22], (DEPTH, PEER_HEADS, 2, PEER_NKEYS, PEER_DQ // 2), (PEER_DQ // 2) ** -0.5),
        'peer_u': nrm(ks[23], (DEPTH, PEER_EXPERTS, D), D ** -0.5),
        'peer_v': nrm(ks[24], (DEPTH, PEER_EXPERTS, D), PEER_HEADS ** -0.5),
    }


def reference(x_prompt, x_sample, state_ret, state_conv, c_prompt, c_sample, ada_w, ada_b, norm_w,
              final_norm_w, ret_w_in, ret_gn_w, ret_w_out, conv_w_pw1, conv_b_pw1, conv_w_dw, conv_b_dw,
              conv_ln_w, conv_ln_b, conv_w_pw2, conv_b_pw2, peer_w_q, peer_keys, peer_u, peer_v):
    B, L = x_prompt.shape[0], x_prompt.shape[1]
    pos_p = jnp.arange(L, dtype=jnp.float32)
    pos_s = PAST_LEN + jnp.arange(x_sample.shape[1], dtype=jnp.float32)
    zero_ret = jnp.zeros((N_RET, B, RET_HEADS, RET_DK, RET_DV), x_prompt.dtype)
    zero_conv = jnp.zeros((N_CONV, B, CONV_WIDTH - 1, D_MODEL), x_prompt.dtype)
    y_prompt, ret_p, conv_p = trunk(x_prompt, c_prompt, zero_ret, zero_conv, pos_p, ada_w, ada_b, norm_w,
                                    final_norm_w, ret_w_in, ret_gn_w, ret_w_out, conv_w_pw1, conv_b_pw1,
                                    conv_w_dw, conv_b_dw, conv_ln_w, conv_ln_b, conv_w_pw2, conv_b_pw2,
                                    peer_w_q, peer_keys, peer_u, peer_v)
    y_sample, ret_s, conv_s = trunk(x_sample, c_sample, state_ret, state_conv, pos_s, ada_w, ada_b, norm_w,
                                    final_norm_w, ret_w_in, ret_gn_w, ret_w_out, conv_w_pw1, conv_b_pw1,
                                    conv_w_dw, conv_b_dw, conv_ln_w, conv_ln_b, conv_w_pw2, conv_b_pw2,
                                    peer_w_q, peer_keys, peer_u, peer_v)
    return (y_prompt, y_sample, ret_p, conv_p, ret_s, conv_s)
```

```python
import functools
import math

import jax
import jax.numpy as jnp
from jax import lax
from jax.experimental import pallas as pl
from jax.experimental.pallas import tpu as pltpu

F32 = jnp.float32
BF16 = jnp.bfloat16

NORM_EPS = 1e-6
ROPE_BASE = 10000.0
RET_CHUNK = 128
PEER_TOPK = 16
PAST_LEN = 16384

LANES = 128
VMEM_LIMIT_BYTES = 56 * 1024 * 1024
ROW_TILE = 512
COL_TILE = 512
EXPERT_TILE = 256
ROUTE_TILE = 256

_NT = (((1,), (1,)), ((), ()))
_TN = (((0,), (0,)), ((), ()))


def _params(n_axes):
    return pltpu.CompilerParams(dimension_semantics=("arbitrary",) * n_axes,
                                vmem_limit_bytes=VMEM_LIMIT_BYTES)


def _silu(x):
    return x * (1.0 / (1.0 + jnp.exp(-x)))


def _mod_spec(mod, tile, chunk, index_of):
    rm = mod.shape[1]
    d = mod.shape[2] // 6
    if rm == 1:
        return pl.BlockSpec((1, 1, d), lambda *ids: (index_of(*ids)[0], 0, chunk))
    return pl.BlockSpec((1, tile, d), lambda *ids: (index_of(*ids)[0], index_of(*ids)[1], chunk))


def _ada_kernel(c_ref, w_ref, b_ref, o_ref):
    sc = _silu(c_ref[...]).astype(BF16)
    o_ref[...] = jnp.dot(sc, w_ref[...].astype(BF16), preferred_element_type=F32) + b_ref[...]


def _ada(c_all, ada_w, ada_b):
    m, d = c_all.shape
    nl, _, n = ada_w.shape
    tn = min(n, 1024)
    return pl.pallas_call(
        _ada_kernel,
        grid=(nl, n // tn),
        in_specs=[pl.BlockSpec((m, d), lambda l, j: (0, 0)),
                  pl.BlockSpec((None, d, tn), lambda l, j: (l, 0, j)),
                  pl.BlockSpec((None, 1, tn), lambda l, j: (l, 0, j))],
        out_specs=pl.BlockSpec((None, m, tn), lambda l, j: (l, 0, j)),
        out_shape=jax.ShapeDtypeStruct((nl, m, n), F32),
        compiler_params=_params(2),
        name="ada_mod",
    )(c_all, ada_w, ada_b[:, None, :])


def _modnorm_kernel(x_ref, w_ref, sh_ref, sc_ref, o_ref):
    x = x_ref[0]
    y = x * lax.rsqrt(jnp.mean(x * x, axis=-1, keepdims=True) + NORM_EPS) * w_ref[...]
    o_ref[0] = (y * (1.0 + sc_ref[0]) + sh_ref[0]).astype(o_ref.dtype)


def _modnorm(x, w, mod, shift_chunk, scale_chunk):
    g, r, d = x.shape
    tr = min(r, ROW_TILE)
    ix = lambda gi, ti: (gi, ti)
    return pl.pallas_call(
        _modnorm_kernel,
        grid=(g, r // tr),
        in_specs=[pl.BlockSpec((1, tr, d), lambda gi, ti: (gi, ti, 0)),
                  pl.BlockSpec((1, d), lambda gi, ti: (0, 0)),
                  _mod_spec(mod, tr, shift_chunk, ix),
                  _mod_spec(mod, tr, scale_chunk, ix)],
        out_specs=pl.BlockSpec((1, tr, d), lambda gi, ti: (gi, ti, 0)),
        out_shape=jax.ShapeDtypeStruct((g, r, d), BF16),
        compiler_params=_params(2),
        name="modnorm",
    )(x, w[None, :], mod, mod)


def _rmsnorm_kernel(x_ref, w_ref, o_ref):
    x = x_ref[0]
    o_ref[0] = x * lax.rsqrt(jnp.mean(x * x, axis=-1, keepdims=True) + NORM_EPS) * w_ref[...]


def _rmsnorm(x, w):
    g, r, d = x.shape
    tr = min(r, ROW_TILE)
    return pl.pallas_call(
        _rmsnorm_kernel,
        grid=(g, r // tr),
        in_specs=[pl.BlockSpec((1, tr, d), lambda gi, ti: (gi, ti, 0)),
                  pl.BlockSpec((1, d), lambda gi, ti: (0, 0))],
        out_specs=pl.BlockSpec((1, tr, d), lambda gi, ti: (gi, ti, 0)),
        out_shape=jax.ShapeDtypeStruct((g, r, d), F32),
        compiler_params=_params(2),
        name="final_rmsnorm",
    )(x, w[None, :])


def _first_inner_step():
    return jnp.logical_and(pl.program_id(1) == 0, pl.program_id(2) == 0)


def _mm_plain_kernel(x_ref, w_ref, o_ref, wbf):
    @pl.when(_first_inner_step())
    def _():
        wbf[...] = w_ref[...].astype(BF16)
    o_ref[0] = jnp.dot(x_ref[0], wbf[...], preferred_element_type=F32)


def _mm_plain(x, w3, layer):
    g, r, k = x.shape
    n = w3.shape[2]
    tm, tn = min(r, ROW_TILE), min(n, COL_TILE)
    return pl.pallas_call(
        _mm_plain_kernel,
        grid=(n // tn, g, r // tm),
        in_specs=[pl.BlockSpec((1, tm, k), lambda j, gi, ti: (gi, ti, 0)),
                  pl.BlockSpec((None, k, tn), lambda j, gi, ti: (layer, 0, j))],
        out_specs=pl.BlockSpec((1, tm, tn), lambda j, gi, ti: (gi, ti, j)),
        out_shape=jax.ShapeDtypeStruct((g, r, n), F32),
        scratch_shapes=[pltpu.VMEM((k, tn), BF16)],
        compiler_params=_params(3),
        name="mm_plain",
    )(x, w3)


def _mm_glu_kernel(x_ref, wa_ref, wb_ref, ba_ref, bb_ref, o_ref, wabf, wbbf):
    @pl.when(_first_inner_step())
    def _():
        wabf[...] = wa_ref[...].astype(BF16)
        wbbf[...] = wb_ref[...].astype(BF16)
    x = x_ref[0]
    a = jnp.dot(x, wabf[...], preferred_element_type=F32) + ba_ref[...]
    b = jnp.dot(x, wbbf[...], preferred_element_type=F32) + bb_ref[...]
    o_ref[0] = a * (1.0 / (1.0 + jnp.exp(-b)))


def _mm_glu(x, w3, b3, layer):
    g, r, k = x.shape
    n = w3.shape[2] // 2
    tm, tn = min(r, ROW_TILE), min(n, COL_TILE)
    nb = n // tn
    return pl.pallas_call(
        _mm_glu_kernel,
        grid=(nb, g, r // tm),
        in_specs=[pl.BlockSpec((1, tm, k), lambda j, gi, ti: (gi, ti, 0)),
                  pl.BlockSpec((None, k, tn), lambda j, gi, ti: (layer, 0, j)),
                  pl.BlockSpec((None, k, tn), lambda j, gi, ti: (layer, 0, nb + j)),
                  pl.BlockSpec((None, 1, tn), lambda j, gi, ti: (layer, 0, j)),
                  pl.BlockSpec((None, 1, tn), lambda j, gi, ti: (layer, 0, nb + j))],
        out_specs=pl.BlockSpec((1, tm, tn), lambda j, gi, ti: (gi, ti, j)),
        out_shape=jax.ShapeDtypeStruct((g, r, n), F32),
        scratch_shapes=[pltpu.VMEM((k, tn), BF16), pltpu.VMEM((k, tn), BF16)],
        compiler_params=_params(3),
        name="mm_glu",
    )(x, w3, w3, b3, b3)


def _mm_resid_kernel(x_ref, w_ref, *rest, has_bias):
    if has_bias:
        b_ref, res_ref, gate_ref, o_ref, wbf = rest
    else:
        res_ref, gate_ref, o_ref, wbf = rest

    @pl.when(_first_inner_step())
    def _():
        wbf[...] = w_ref[...].astype(BF16)
    y = jnp.dot(x_ref[0], wbf[...], preferred_element_type=F32)
    if has_bias:
        y = y + b_ref[...]
    o_ref[0] = res_ref[0] + gate_ref[0] * y


def _mm_resid(x, w3, b3, layer, res, mod, gate_chunk):
    g, r, k = x.shape
    n = w3.shape[2]
    tm, tn = min(r, ROW_TILE), min(n, COL_TILE)
    per_d = n // tn
    rm = mod.shape[1]
    if rm == 1:
        gate_spec = pl.BlockSpec((1, 1, tn), lambda j, gi, ti: (gi, 0, gate_chunk * per_d + j))
    else:
        gate_spec = pl.BlockSpec((1, tm, tn), lambda j, gi, ti: (gi, ti, gate_chunk * per_d + j))
    in_specs = [pl.BlockSpec((1, tm, k), lambda j, gi, ti: (gi, ti, 0)),
                pl.BlockSpec((None, k, tn), lambda j, gi, ti: (layer, 0, j))]
    args = [x, w3]
    if b3 is not None:
        in_specs.append(pl.BlockSpec((None, 1, tn), lambda j, gi, ti: (layer, 0, j)))
        args.append(b3)
    in_specs += [pl.BlockSpec((1, tm, tn), lambda j, gi, ti: (gi, ti, j)), gate_spec]
    args += [res, mod]
    return pl.pallas_call(
        functools.partial(_mm_resid_kernel, has_bias=b3 is not None),
        grid=(n // tn, g, r // tm),
        in_specs=in_specs,
        out_specs=pl.BlockSpec((1, tm, tn), lambda j, gi, ti: (gi, ti, j)),
        out_shape=jax.ShapeDtypeStruct((g, r, n), F32),
        scratch_shapes=[pltpu.VMEM((k, tn), BF16)],
        compiler_params=_params(3),
        name="mm_resid",
    )(*args)


def _rope(x, cos, sin, half):
    x1, x2 = x[:, :half], x[:, half:]
    return jnp.concatenate([x1 * cos - x2 * sin, x1 * sin + x2 * cos], axis=-1)


def _group_norm_gate(o, gn_w, g):
    mu = jnp.mean(o, axis=-1, keepdims=True)
    d = o - mu
    var = jnp.mean(d * d, axis=-1, keepdims=True)
    return d * lax.rsqrt(var + NORM_EPS) * gn_w * _silu(g)


def _ret_prompt_kernel(q_ref, k_ref, v_ref, g_ref, cos_ref, sin_ref, dm_ref, xi_ref, zt_ref, dec_ref, gn_ref,
                       og_ref, r_ref, *, half, scale):
    @pl.when(pl.program_id(2) == 0)
    def _():
        r_ref[0, 0] = jnp.zeros(r_ref.shape[2:], F32)

    cos, sin = cos_ref[...], sin_ref[...]
    q = _rope(q_ref[0], cos, sin, half)
    k = _rope(k_ref[0], cos, sin, half) * scale
    vb = v_ref[0].astype(BF16)
    r = r_ref[0, 0]
    sc = lax.dot_general(q.astype(BF16), k.astype(BF16), _NT, preferred_element_type=F32) * dm_ref[0]
    o = (jnp.dot(sc.astype(BF16), vb, preferred_element_type=F32)
         + jnp.dot((q * xi_ref[0]).astype(BF16), r.astype(BF16), preferred_element_type=F32))
    r_ref[0, 0] = r * dec_ref[0] + lax.dot_general((k * zt_ref[0]).astype(BF16), vb, _TN,
                                                   preferred_element_type=F32)
    og_ref[0] = _group_norm_gate(o, gn_ref[...], g_ref[0]).astype(BF16)


def _retention_tables(n_heads, chunk, dk, dv):
    lg = jnp.log(1.0 - 2.0 ** (-5.0 - jnp.arange(n_heads, dtype=F32)))
    idx = jnp.arange(chunk, dtype=F32)
    diff = idx[:, None] - idx[None, :]
    dmask = jnp.where(diff[None] >= 0, jnp.exp(jnp.maximum(diff, 0.0)[None] * lg[:, None, None]), 0.0)
    xi = jnp.exp((idx + 1.0)[None, :] * lg[:, None])
    zeta = jnp.exp((chunk - 1.0 - idx)[None, :] * lg[:, None])
    dec = jnp.exp(chunk * lg)
    return (dmask,
            jnp.broadcast_to(xi[:, :, None], (n_heads, chunk, dk)),
            jnp.broadcast_to(zeta[:, :, None], (n_heads, chunk, dk)),
            jnp.broadcast_to(dec[:, None, None], (n_heads, 1, dv)))


def _rope_tables(pos, half):
    inv = ROPE_BASE ** (-jnp.arange(half, dtype=F32) / half)
    ang = pos[:, None] * inv[None, :]
    return jnp.cos(ang), jnp.sin(ang)


def _ret_prompt(proj, gn_w, n_heads, dk, dv):
    b, l, _ = proj.shape
    d = n_heads * dk
    chunk = math.gcd(l, RET_CHUNK)
    half = dk // 2
    cos, sin = _rope_tables(jnp.arange(l, dtype=F32), half)
    dmask, xiq, ztk, dec = _retention_tables(n_heads, chunk, dk, dv)
    kq, kv = d // dk, (2 * d) // dv
    og, r = pl.pallas_call(
        functools.partial(_ret_prompt_kernel, half=half, scale=dk ** -0.5),
        grid=(b, n_heads, l // chunk),
        in_specs=[pl.BlockSpec((1, chunk, dk), lambda bi, h, c: (bi, c, h)),
                  pl.BlockSpec((1, chunk, dk), lambda bi, h, c: (bi, c, kq + h)),
                  pl.BlockSpec((1, chunk, dv), lambda bi, h, c: (bi, c, kv + h)),
                  pl.BlockSpec((1, chunk, dv), lambda bi, h, c: (bi, c, 2 * kv + h)),
                  pl.BlockSpec((chunk, half), lambda bi, h, c: (c, 0)),
                  pl.BlockSpec((chunk, half), lambda bi, h, c: (c, 0)),
                  pl.BlockSpec((1, chunk, chunk), lambda bi, h, c: (h, 0, 0)),
                  pl.BlockSpec((1, chunk, dk), lambda bi, h, c: (h, 0, 0)),
                  pl.BlockSpec((1, chunk, dk), lambda bi, h, c: (h, 0, 0)),
                  pl.BlockSpec((1, 1, dv), lambda bi, h, c: (h, 0, 0)),
                  pl.BlockSpec((1, dv), lambda bi, h, c: (0, h))],
        out_specs=[pl.BlockSpec((1, chunk, dv), lambda bi, h, c: (bi, c, h)),
                   pl.BlockSpec((1, 1, dk, dv), lambda bi, h, c: (bi, h, 0, 0))],
        out_shape=[jax.ShapeDtypeStruct((b, l, n_heads * dv), BF16),
                   jax.ShapeDtypeStruct((b, n_heads, dk, dv), F32)],
        compiler_params=_params(3),
        name="retention_prompt",
    )(proj, proj, proj, proj, cos, sin, dmask, xiq, ztk, dec, gn_w[None, :])
    return og, r


def _ret_sample_kernel(q_ref, k_ref, v_ref, g_ref, st_ref, cos_ref, sin_ref, xi_ref, dec_ref, gn_ref,
                       og_ref, rn_ref, qpad, kpad, o_sc, *, n_heads, half, scale):
    @pl.when(pl.program_id(0) == 0)
    def _():
        qpad[...] = jnp.zeros(qpad.shape, F32)
        kpad[...] = jnp.zeros(kpad.shape, F32)

    cos, sin = cos_ref[...], sin_ref[...]
    q = _rope(q_ref[0], cos, sin, half)
    k = _rope(k_ref[0], cos, sin, half) * scale
    qpad[0:n_heads, :] = q
    kpad[0:n_heads, :] = k
    q_t = qpad[...].T
    k_t = kpad[...].T
    qk = jnp.sum(q * k, axis=-1, keepdims=True)
    v = v_ref[0]
    for h in range(n_heads):
        rh = st_ref[0, h]
        vh = v[h:h + 1, :]
        qr = jnp.sum(rh * q_t[:, h:h + 1], axis=0, keepdims=True)
        o_sc[h:h + 1, :] = qk[h:h + 1, :] * vh + xi_ref[h:h + 1, :] * qr
        rn_ref[0, h] = rh * dec_ref[h:h + 1, :] + k_t[:, h:h + 1] * vh
    og_ref[0] = _group_norm_gate(o_sc[...], gn_ref[...], g_ref[0]).astype(BF16)


def _ret_sample(proj, state, j, prev_new_state, gn_w, n_heads, dk, dv):
    _, bs, n6 = proj.shape
    half = dk // 2
    cos, sin = _rope_tables(jnp.full((1,), PAST_LEN, F32), half)
    lg = jnp.log(1.0 - 2.0 ** (-5.0 - jnp.arange(n_heads, dtype=F32)))
    gamma = jnp.broadcast_to(jnp.exp(lg)[:, None], (n_heads, dv))
    qk_view = proj.reshape(bs, n6 // dk, dk)
    vg_view = proj.reshape(bs, n6 // dv, dv)
    in_specs = [pl.BlockSpec((1, n_heads, dk), lambda b: (b, 0, 0)),
                pl.BlockSpec((1, n_heads, dk), lambda b: (b, 1, 0)),
                pl.BlockSpec((1, n_heads, dv), lambda b: (b, 1, 0)),
                pl.BlockSpec((1, n_heads, dv), lambda b: (b, 2, 0)),
                pl.BlockSpec((None, 1, n_heads, dk, dv), lambda b: (j, b, 0, 0, 0)),
                pl.BlockSpec((1, half), lambda b: (0, 0)),
                pl.BlockSpec((1, half), lambda b: (0, 0)),
                pl.BlockSpec((n_heads, dv), lambda b: (0, 0)),
                pl.BlockSpec((n_heads, dv), lambda b: (0, 0)),
                pl.BlockSpec((n_heads, dv), lambda b: (0, 0))]
    args = [qk_view, qk_view, vg_view, vg_view, state, cos, sin, gamma, gamma, gn_w.reshape(n_heads, dv)]
    aliases = {}
    kern = functools.partial(_ret_sample_kernel, n_heads=n_heads, half=half, scale=dk ** -0.5)
    if prev_new_state is not None:
        in_specs.append(pl.BlockSpec(memory_space=pl.ANY))
        args.append(prev_new_state)
        aliases = {len(args) - 1: 1}
        inner = kern
        kern = lambda *refs: inner(*refs[:10], *refs[11:])
    og, new_state = pl.pallas_call(
        kern,
        grid=(bs,),
        in_specs=in_specs,
        out_specs=[pl.BlockSpec((1, n_heads, dv), lambda b: (b, 0, 0)),
                   pl.BlockSpec((None, 1, n_heads, dk, dv), lambda b: (j, b, 0, 0, 0))],
        out_shape=[jax.ShapeDtypeStruct((bs, n_heads, dv), BF16),
                   jax.ShapeDtypeStruct(state.shape, state.dtype)],
        scratch_shapes=[pltpu.VMEM((LANES, dk), F32), pltpu.VMEM((LANES, dk), F32),
                        pltpu.VMEM((n_heads, dv), F32)],
        input_output_aliases=aliases,
        compiler_params=_params(1),
        name="retention_sample",
    )(*args)
    return og.reshape(1, bs, n_heads * dv), new_state


def _layer_norm_silu(y, w, b):
    mu = jnp.mean(y, axis=-1, keepdims=True)
    d = y - mu
    var = jnp.mean(d * d, axis=-1, keepdims=True)
    return _silu(d * lax.rsqrt(var + NORM_EPS) * w + b)


def _conv_prompt_kernel(u_ref, wdw_ref, bdw_ref, lnw_ref, lnb_ref, z_ref, nc_ref, ext, *, width, tl, pad):
    li = pl.program_id(1)

    @pl.when(li == 0)
    def _():
        ext[0:pad, :] = jnp.zeros((pad, ext.shape[1]), F32)

    @pl.when(li > 0)
    def _():
        ext[0:pad, :] = ext[tl:tl + pad, :]

    ext[pad:pad + tl, :] = u_ref[0]
    off = pad - (width - 1)
    y = bdw_ref[...] + wdw_ref[0:1, :] * ext[off:off + tl, :]
    for w in range(1, width):
        y = y + wdw_ref[w:w + 1, :] * ext[off + w:off + w + tl, :]
    z_ref[0] = _layer_norm_silu(y, lnw_ref[...], lnb_ref[...]).astype(BF16)

    @pl.when(li == pl.num_programs(1) - 1)
    def _():
        nc_ref[0] = ext[tl + off:tl + pad, :]


def _conv_prompt(u, w_dw, b_dw, ln_w, ln_b, j):
    b, l, d = u.shape
    width = w_dw.shape[1]
    tl = min(l, 128)
    pad = 32
    vec = lambda a: a[:, None, :]
    return pl.pallas_call(
        functools.partial(_conv_prompt_kernel, width=width, tl=tl, pad=pad),
        grid=(b, l // tl),
        in_specs=[pl.BlockSpec((1, tl, d), lambda bi, li: (bi, li, 0)),
                  pl.BlockSpec((None, width, d), lambda bi, li: (j, 0, 0)),
                  pl.BlockSpec((None, 1, d), lambda bi, li: (j, 0, 0)),
                  pl.BlockSpec((None, 1, d), lambda bi, li: (j, 0, 0)),
                  pl.BlockSpec((None, 1, d), lambda bi, li: (j, 0, 0))],
        out_specs=[pl.BlockSpec((1, tl, d), lambda bi, li: (bi, li, 0)),
                   pl.BlockSpec((1, width - 1, d), lambda bi, li: (bi, 0, 0))],
        out_shape=[jax.ShapeDtypeStruct((b, l, d), BF16),
                   jax.ShapeDtypeStruct((b, width - 1, d), F32)],
        scratch_shapes=[pltpu.VMEM((tl + pad, d), F32)],
        compiler_params=_params(2),
        name="conv_prompt",
    )(u, w_dw, vec(b_dw), vec(ln_w), vec(ln_b))


def _conv_sample_kernel(st_ref, u_ref, wdw_ref, bdw_ref, lnw_ref, lnb_ref, z_ref, nc_ref, *, width):
    st = st_ref[...]
    u = u_ref[...]
    y = (jnp.sum(st * wdw_ref[0:width - 1, :], axis=0, keepdims=True)
         + u * wdw_ref[width - 1:width, :] + bdw_ref[...])
    z_ref[...] = _layer_norm_silu(y, lnw_ref[...], lnb_ref[...]).astype(BF16)
    nc_ref[0:width - 2, :] = st[1:width - 1, :]
    nc_ref[width - 2:width - 1, :] = u


def _conv_sample(u, state, w_dw, b_dw, ln_w, ln_b, j):
    _, bs, d = u.shape
    width = w_dw.shape[1]
    vec = lambda a: a[:, None, :]
    z, nc = pl.pallas_call(
        functools.partial(_conv_sample_kernel, width=width),
        grid=(bs,),
        in_specs=[pl.BlockSpec((None, None, width - 1, d), lambda b: (j, b, 0, 0)),
                  pl.BlockSpec((None, 1, d), lambda b: (b, 0, 0)),
                  pl.BlockSpec((None, width, d), lambda b: (j, 0, 0)),
                  pl.BlockSpec((None, 1, d), lambda b: (j, 0, 0)),
                  pl.BlockSpec((None, 1, d), lambda b: (j, 0, 0)),
                  pl.BlockSpec((None, 1, d), lambda b: (j, 0, 0))],
        out_specs=[pl.BlockSpec((None, 1, d), lambda b: (b, 0, 0)),
                   pl.BlockSpec((None, width - 1, d), lambda b: (b, 0, 0))],
        out_shape=[jax.ShapeDtypeStruct((bs, 1, d), BF16),
                   jax.ShapeDtypeStruct((bs, width - 1, d), state.dtype)],
        compiler_params=_params(1),
        name="conv_sample",
    )(state, u.reshape(bs, 1, d), w_dw, vec(b_dw), vec(ln_w), vec(ln_b))
    return z.reshape(1, bs, d), nc


def _top_rows(s, count):
    rows = []
    for _ in range(count):
        m = jnp.max(s, axis=0, keepdims=True)
        rows.append(m)
        s = jnp.where(s == m, -jnp.inf, s)
    return rows


def _route_kernel(h_ref, wq_ref, keys_ref, s1_ref, s2_ref, e1_ref, e2_ref, thr_ref, b_sc, cand_sc, *, half, topk):
    q = jnp.dot(h_ref[0], wq_ref[...].astype(BF16), preferred_element_type=F32).astype(BF16)
    s1 = lax.dot_general(keys_ref[0, 0].astype(BF16), q[:, :half], _NT, preferred_element_type=F32)
    s2 = lax.dot_general(keys_ref[0, 1].astype(BF16), q[:, half:], _NT, preferred_element_type=F32)
    a = _top_rows(s1, topk)
    b = _top_rows(s2, topk)
    for r in range(topk):
        b_sc[r:r + 1, :] = b[r]
    bmat = b_sc[...]
    for r in range(topk):
        cand_sc[r * topk:(r + 1) * topk, :] = a[r] + bmat
    cand = cand_sc[...]
    thr = _top_rows(cand, topk)[topk - 1]
    z = jnp.sum(jnp.where(cand >= thr, jnp.exp(cand - (a[0] + b[0])), 0.0), axis=0, keepdims=True)
    s1_ref[0, 0] = s1
    s2_ref[0, 0] = s2
    e1_ref[0, 0] = jnp.exp(s1 - a[0]) * (1.0 / z)
    e2_ref[0, 0] = jnp.exp(s2 - b[0])
    thr_ref[0, 0] = thr


def _route(h, w_q, keys, layer):
    g, r, d = h.shape
    _, hp, _, nk, half = keys.shape
    dq = 2 * half
    tr = min(r, ROUTE_TILE)
    tab = jax.ShapeDtypeStruct((g, hp, nk, r), F32)
    tab_spec = pl.BlockSpec((1, 1, nk, tr), lambda gi, ti, hi: (gi, hi, 0, ti))
    return pl.pallas_call(
        functools.partial(_route_kernel, half=half, topk=PEER_TOPK),
        grid=(g, r // tr, hp),
        in_specs=[pl.BlockSpec((1, tr, d), lambda gi, ti, hi: (gi, ti, 0)),
                  pl.BlockSpec((None, d, dq), lambda gi, ti, hi: (layer, 0, hi)),
                  pl.BlockSpec((None, 1, 2, nk, half), lambda gi, ti, hi: (layer, hi, 0, 0, 0))],
        out_specs=[tab_spec, tab_spec, tab_spec, tab_spec,
                   pl.BlockSpec((1, 1, 1, tr), lambda gi, ti, hi: (gi, hi, 0, ti))],
        out_shape=[tab, tab, tab, tab, jax.ShapeDtypeStruct((g, hp, 1, r), F32)],
        scratch_shapes=[pltpu.VMEM((PEER_TOPK, tr), F32), pltpu.VMEM((PEER_TOPK * PEER_TOPK, tr), F32)],
        compiler_params=_params(3),
        name="peer_route",
    )(h, w_q, keys)


def _gelu(x):
    return 0.5 * x * (1.0 + lax.erf(x * (2.0 ** -0.5)))


def _peer_dense_kernel(h_ref, u_ref, v_ref, s1_ref, s2_ref, e1_ref, e2_ref, thr_ref, x_ref, gate_ref,
                       o_ref, a_sc, w_sc, *, n_heads, nk, tt, eb):
    e = pl.program_id(2)

    @pl.when(e == 0)
    def _():
        o_ref[0] = jnp.zeros(o_ref.shape[1:], F32)

    a_sc[...] = lax.dot_general(u_ref[...].astype(BF16), h_ref[0], _NT, preferred_element_type=F32)

    rows_per = 16
    for lt in range(tt // LANES):
        lanes = pl.ds(lt * LANES, LANES)
        thr_b = [jnp.broadcast_to(thr_ref[0, h, :, lanes], (rows_per, LANES)) for h in range(n_heads)]

        def i_body(ii, carry):
            i_abs = e * (eb // nk) + ii
            grp = pl.ds(pl.multiple_of((i_abs // 8) * 8, 8), 8)
            pick = lax.broadcasted_iota(jnp.int32, (8, LANES), 0) == i_abs % 8

            def row_of(ref, h):
                row = jnp.sum(jnp.where(pick, ref[0, h, grp, lanes], 0.0), axis=0, keepdims=True)
                return jnp.broadcast_to(row, (rows_per, LANES))

            s1_b = [row_of(s1_ref, h) for h in range(n_heads)]
            e1_b = [row_of(e1_ref, h) for h in range(n_heads)]
            for jc in range(nk // rows_per):
                jrows = pl.ds(jc * rows_per, rows_per)
                rows = pl.ds(pl.multiple_of(ii * nk + jc * rows_per, rows_per), rows_per)
                gsum = jnp.zeros((rows_per, LANES), F32)
                for h in range(n_heads):
                    hit = s1_b[h] + s2_ref[0, h, jrows, lanes] >= thr_b[h]
                    gsum = gsum + jnp.where(hit, e1_b[h] * e2_ref[0, h, jrows, lanes], 0.0)
                w_sc[rows, lanes] = (gsum * _gelu(a_sc[rows, lanes])).astype(BF16)
            return carry

        lax.fori_loop(0, eb // nk, i_body, 0)

    o_ref[0] += lax.dot_general(w_sc[...], v_ref[...].astype(BF16), _TN, preferred_element_type=F32)

    @pl.when(e == pl.num_programs(2) - 1)
    def _():
        o_ref[0] = x_ref[0] + gate_ref[0] * o_ref[0]


def _peer_dense(h, u, v, layer, tables, x, mod, gate_chunk):
    g, r, d = h.shape
    s1, s2, e1, e2, thr = tables
    _, hp, nk, _ = s1.shape
    n_exp = u.shape[1]
    tt = min(r, ROW_TILE)
    eb = EXPERT_TILE
    tab_spec = pl.BlockSpec((1, hp, nk, tt), lambda gi, ti, ei: (gi, 0, 0, ti))
    return pl.pallas_call(
        functools.partial(_peer_dense_kernel, n_heads=hp, nk=nk, tt=tt, eb=eb),
        grid=(g, r // tt, n_exp // eb),
        in_specs=[pl.BlockSpec((1, tt, d), lambda gi, ti, ei: (gi, ti, 0)),
                  pl.BlockSpec((None, eb, d), lambda gi, ti, ei: (layer, ei, 0)),
                  pl.BlockSpec((None, eb, d), lambda gi, ti, ei: (layer, ei, 0)),
                  tab_spec, tab_spec, tab_spec, tab_spec,
                  pl.BlockSpec((1, hp, 1, tt), lambda gi, ti, ei: (gi, 0, 0, ti)),
                  pl.BlockSpec((1, tt, d), lambda gi, ti, ei: (gi, ti, 0)),
                  _mod_spec(mod, tt, gate_chunk, lambda gi, ti, ei: (gi, ti))],
        out_specs=pl.BlockSpec((1, tt, d), lambda gi, ti, ei: (gi, ti, 0)),
        out_shape=jax.ShapeDtypeStruct((g, r, d), F32),
        scratch_shapes=[pltpu.VMEM((eb, tt), F32), pltpu.VMEM((eb, tt), BF16)],
        compiler_params=_params(3),
        name="peer_dense",
    )(h, u, v, s1, s2, e1, e2, thr, x, mod)


def _trunk(x, mods, is_prompt, state_ret, state_conv, norm_w, final_norm_w,
           ret_w_in, ret_gn_w, ret_w_out, conv_w_pw1, conv_b_pw1, conv_w_dw, conv_b_dw,
           conv_ln_w, conv_ln_b, conv_w_pw2, conv_b_pw2, peer_w_q, peer_keys, peer_u, peer_v):
    depth = norm_w.shape[0]
    n_heads, dk, dv = state_ret.shape[2:]
    new_ret, new_conv = None, []
    ret_prompt_states = []
    vec = lambda a: a[:, None, :]
    for i in range(depth):
        mod = mods[i]
        h = _modnorm(x, norm_w[i, 0], mod, 0, 1)
        j = i // 2
        if i % 2 == 0:
            proj = _mm_plain(h, ret_w_in, j)
            if is_prompt:
                og, r = _ret_prompt(proj, ret_gn_w[j], n_heads, dk, dv)
                ret_prompt_states.append(r)
            else:
                og, new_ret = _ret_sample(proj, state_ret, j, new_ret, ret_gn_w[j], n_heads, dk, dv)
            x = _mm_resid(og, ret_w_out, None, j, x, mod, 2)
        else:
            u = _mm_glu(h, conv_w_pw1, vec(conv_b_pw1), j)
            if is_prompt:
                z, nc = _conv_prompt(u, conv_w_dw, conv_b_dw, conv_ln_w, conv_ln_b, j)
            else:
                z, nc = _conv_sample(u, state_conv, conv_w_dw, conv_b_dw, conv_ln_w, conv_ln_b, j)
            new_conv.append(nc)
            x = _mm_resid(z, conv_w_pw2, vec(conv_b_pw2), j, x, mod, 2)
        h = _modnorm(x, norm_w[i, 1], mod, 3, 4)
        tables = _route(h, peer_w_q, peer_keys, i)
        x = _peer_dense(h, peer_u, peer_v, i, tables, x, mod, 5)
    y = _rmsnorm(x, final_norm_w)
    if is_prompt:
        new_ret = jnp.stack(ret_prompt_states)
    return y, new_ret, jnp.stack(new_conv)


def kernel(x_prompt, x_sample, state_ret, state_conv, c_prompt, c_sample, ada_w, ada_b, norm_w, final_norm_w,
           ret_w_in, ret_gn_w, ret_w_out, conv_w_pw1, conv_b_pw1, conv_w_dw, conv_b_dw, conv_ln_w, conv_ln_b,
           conv_w_pw2, conv_b_pw2, peer_w_q, peer_keys, peer_u, peer_v):
    b, l, d = x_prompt.shape
    bs, ls, _ = x_sample.shape
    assert ls == 1, "the sample path handles one new token per sequence"
    depth = ada_w.shape[0]

    n_c = b + bs
    n_pad = -n_c % 8
    c_all = jnp.concatenate([c_prompt, c_sample, jnp.zeros((n_pad, d), c_prompt.dtype)], axis=0)
    mod = _ada(c_all, ada_w, ada_b)
    mods_p = [mod[i, :b].reshape(b, 1, 6 * d) for i in range(depth)]
    mods_s = [mod[i, b:b + bs].reshape(1, bs, 6 * d) for i in range(depth)]

    weights = (norm_w, final_norm_w, ret_w_in, ret_gn_w, ret_w_out, conv_w_pw1, conv_b_pw1, conv_w_dw, conv_b_dw,
               conv_ln_w, conv_ln_b, conv_w_pw2, conv_b_pw2, peer_w_q, peer_keys, peer_u, peer_v)
    y_p, ret_p, conv_p = _trunk(x_prompt, mods_p, True, state_ret, state_conv, *weights)
    y_s, ret_s, conv_s = _trunk(x_sample.reshape(1, bs, d), mods_s, False, state_ret, state_conv, *weights)
    return (y_p, y_s.reshape(bs, 1, d), ret_p, conv_p, ret_s, conv_s)
```

```python
import functools
import math

import jax
import jax.numpy as jnp
from jax import lax
from jax.experimental import pallas as pl
from jax.experimental.pallas import tpu as pltpu

F32 = jnp.float32
BF16 = jnp.bfloat16

NORM_EPS = 1e-6
ROPE_BASE = 10000.0
RET_CHUNK = 128
PEER_TOPK = 16
PAST_LEN = 16384

LANES = 128
VMEM_LIMIT_BYTES = 56 * 1024 * 1024
ROW_TILE = 512
COL_TILE = 512
EXPERT_TILE = 1024
ROUTE_TILE = 512

_NT = (((1,), (1,)), ((), ()))
_TN = (((0,), (0,)), ((), ()))


def _params(n_axes):
    return pltpu.CompilerParams(dimension_semantics=("arbitrary",) * n_axes,
                                vmem_limit_bytes=VMEM_LIMIT_BYTES)


def _silu(x):
    return x * (1.0 / (1.0 + jnp.exp(-x)))


def _mod_spec(mod, tile, chunk, index_of):
    rm = mod.shape[1]
    d = mod.shape[2] // 6
    if rm == 1:
        return pl.BlockSpec((1, 1, d), lambda *ids: (index_of(*ids)[0], 0, chunk))
    return pl.BlockSpec((1, tile, d), lambda *ids: (index_of(*ids)[0], index_of(*ids)[1], chunk))


def _ada_kernel(c_ref, w_ref, b_ref, o_ref):
    sc = _silu(c_ref[...]).astype(BF16)
    o_ref[...] = jnp.dot(sc, w_ref[...].astype(BF16), preferred_element_type=F32) + b_ref[...]


def _ada(c_all, ada_w, ada_b):
    m, d = c_all.shape
    nl, _, n = ada_w.shape
    tn = min(n, 1024)
    return pl.pallas_call(
        _ada_kernel,
        grid=(nl, n // tn),
        in_specs=[pl.BlockSpec((m, d), lambda l, j: (0, 0)),
                  pl.BlockSpec((None, d, tn), lambda l, j: (l, 0, j)),
                  pl.BlockSpec((None, 1, tn), lambda l, j: (l, 0, j))],
        out_specs=pl.BlockSpec((None, m, tn), lambda l, j: (l, 0, j)),
        out_shape=jax.ShapeDtypeStruct((nl, m, n), F32),
        compiler_params=_params(2),
        name="ada_mod",
    )(c_all, ada_w, ada_b[:, None, :])


def _modnorm_kernel(x_ref, w_ref, sh_ref, sc_ref, o_ref):
    x = x_ref[0]
    y = x * lax.rsqrt(jnp.mean(x * x, axis=-1, keepdims=True) + NORM_EPS) * w_ref[...]
    o_ref[0] = (y * (1.0 + sc_ref[0]) + sh_ref[0]).astype(o_ref.dtype)


def _modnorm(x, w, mod, shift_chunk, scale_chunk):
    g, r, d = x.shape
    tr = min(r, ROW_TILE)
    ix = lambda gi, ti: (gi, ti)
    return pl.pallas_call(
        _modnorm_kernel,
        grid=(g, r // tr),
        in_specs=[pl.BlockSpec((1, tr, d), lambda gi, ti: (gi, ti, 0)),
                  pl.BlockSpec((1, d), lambda gi, ti: (0, 0)),
                  _mod_spec(mod, tr, shift_chunk, ix),
                  _mod_spec(mod, tr, scale_chunk, ix)],
        out_specs=pl.BlockSpec((1, tr, d), lambda gi, ti: (gi, ti, 0)),
        out_shape=jax.ShapeDtypeStruct((g, r, d), BF16),
        compiler_params=_params(2),
        name="modnorm",
    )(x, w[None, :], mod, mod)


def _rmsnorm_kernel(x_ref, w_ref, o_ref):
    x = x_ref[0]
    o_ref[0] = x * lax.rsqrt(jnp.mean(x * x, axis=-1, keepdims=True) + NORM_EPS) * w_ref[...]


def _rmsnorm(x, w):
    g, r, d = x.shape
    tr = min(r, ROW_TILE)
    return pl.pallas_call(
        _rmsnorm_kernel,
        grid=(g, r // tr),
        in_specs=[pl.BlockSpec((1, tr, d), lambda gi, ti: (gi, ti, 0)),
                  pl.BlockSpec((1, d), lambda gi, ti: (0, 0))],
        out_specs=pl.BlockSpec((1, tr, d), lambda gi, ti: (gi, ti, 0)),
        out_shape=jax.ShapeDtypeStruct((g, r, d), F32),
        compiler_params=_params(2),
        name="final_rmsnorm",
    )(x, w[None, :])


def _first_inner_step():
    return jnp.logical_and(pl.program_id(1) == 0, pl.program_id(2) == 0)


def _mm_plain_kernel(x_ref, w_ref, o_ref, wbf):
    @pl.when(_first_inner_step())
    def _():
        wbf[...] = w_ref[...].astype(BF16)
    o_ref[0] = jnp.dot(x_ref[0], wbf[...], preferred_element_type=F32)


def _mm_plain(x, w3, layer):
    g, r, k = x.shape
    n = w3.shape[2]
    tm, tn = min(r, ROW_TILE), min(n, 2 * COL_TILE)
    return pl.pallas_call(
        _mm_plain_kernel,
        grid=(n // tn, g, r // tm),
        in_specs=[pl.BlockSpec((1, tm, k), lambda j, gi, ti: (gi, ti, 0)),
                  pl.BlockSpec((None, k, tn), lambda j, gi, ti: (layer, 0, j))],
        out_specs=pl.BlockSpec((1, tm, tn), lambda j, gi, ti: (gi, ti, j)),
        out_shape=jax.ShapeDtypeStruct((g, r, n), F32),
        scratch_shapes=[pltpu.VMEM((k, tn), BF16)],
        compiler_params=_params(3),
        name="mm_plain",
    )(x, w3)


def _mm_glu_kernel(x_ref, wa_ref, wb_ref, ba_ref, bb_ref, o_ref, wabf, wbbf):
    @pl.when(_first_inner_step())
    def _():
        wabf[...] = wa_ref[...].astype(BF16)
        wbbf[...] = wb_ref[...].astype(BF16)
    x = x_ref[0]
    a = jnp.dot(x, wabf[...], preferred_element_type=F32) + ba_ref[...]
    b = jnp.dot(x, wbbf[...], preferred_element_type=F32) + bb_ref[...]
    o_ref[0] = a * (1.0 / (1.0 + jnp.exp(-b)))


def _mm_glu(x, w3, b3, layer):
    g, r, k = x.shape
    n = w3.shape[2] // 2
    tm, tn = min(r, ROW_TILE), min(n, COL_TILE)
    nb = n // tn
    return pl.pallas_call(
        _mm_glu_kernel,
        grid=(nb, g, r // tm),
        in_specs=[pl.BlockSpec((1, tm, k), lambda j, gi, ti: (gi, ti, 0)),
                  pl.BlockSpec((None, k, tn), lambda j, gi, ti: (layer, 0, j)),
                  pl.BlockSpec((None, k, tn), lambda j, gi, ti: (layer, 0, nb + j)),
                  pl.BlockSpec((None, 1, tn), lambda j, gi, ti: (layer, 0, j)),
                  pl.BlockSpec((None, 1, tn), lambda j, gi, ti: (layer, 0, nb + j))],
        out_specs=pl.BlockSpec((1, tm, tn), lambda j, gi, ti: (gi, ti, j)),
        out_shape=jax.ShapeDtypeStruct((g, r, n), F32),
        scratch_shapes=[pltpu.VMEM((k, tn), BF16), pltpu.VMEM((k, tn), BF16)],
        compiler_params=_params(3),
        name="mm_glu",
    )(x, w3, w3, b3, b3)


def _mm_resid_kernel(x_ref, w_ref, *rest, has_bias):
    if has_bias:
        b_ref, res_ref, gate_ref, o_ref, wbf = rest
    else:
        res_ref, gate_ref, o_ref, wbf = rest

    @pl.when(_first_inner_step())
    def _():
        wbf[...] = w_ref[...].astype(BF16)
    y = jnp.dot(x_ref[0], wbf[...], preferred_element_type=F32)
    if has_bias:
        y = y + b_ref[...]
    o_ref[0] = res_ref[0] + gate_ref[0] * y


def _mm_resid(x, w3, b3, layer, res, mod, gate_chunk):
    g, r, k = x.shape
    n = w3.shape[2]
    tm, tn = min(r, ROW_TILE), min(n, COL_TILE)
    per_d = n // tn
    rm = mod.shape[1]
    if rm == 1:
        gate_spec = pl.BlockSpec((1, 1, tn), lambda j, gi, ti: (gi, 0, gate_chunk * per_d + j))
    else:
        gate_spec = pl.BlockSpec((1, tm, tn), lambda j, gi, ti: (gi, ti, gate_chunk * per_d + j))
    in_specs = [pl.BlockSpec((1, tm, k), lambda j, gi, ti: (gi, ti, 0)),
                pl.BlockSpec((None, k, tn), lambda j, gi, ti: (layer, 0, j))]
    args = [x, w3]
    if b3 is not None:
        in_specs.append(pl.BlockSpec((None, 1, tn), lambda j, gi, ti: (layer, 0, j)))
        args.append(b3)
    in_specs += [pl.BlockSpec((1, tm, tn), lambda j, gi, ti: (gi, ti, j)), gate_spec]
    args += [res, mod]
    return pl.pallas_call(
        functools.partial(_mm_resid_kernel, has_bias=b3 is not None),
        grid=(n // tn, g, r // tm),
        in_specs=in_specs,
        out_specs=pl.BlockSpec((1, tm, tn), lambda j, gi, ti: (gi, ti, j)),
        out_shape=jax.ShapeDtypeStruct((g, r, n), F32),
        scratch_shapes=[pltpu.VMEM((k, tn), BF16)],
        compiler_params=_params(3),
        name="mm_resid",
    )(*args)


def _rope(x, cos, sin, half):
    x1, x2 = x[:, :half], x[:, half:]
    return jnp.concatenate([x1 * cos - x2 * sin, x1 * sin + x2 * cos], axis=-1)


def _group_norm_gate(o, gn_w, g):
    mu = jnp.mean(o, axis=-1, keepdims=True)
    d = o - mu
    var = jnp.mean(d * d, axis=-1, keepdims=True)
    return d * lax.rsqrt(var + NORM_EPS) * gn_w * _silu(g)


def _ret_prompt_kernel(q_ref, k_ref, v_ref, g_ref, cos_ref, sin_ref, dm_ref, xi_ref, zt_ref, dec_ref, gn_ref,
                       og_ref, r_ref, *, n_heads, dk, dv, scale):
    @pl.when(pl.program_id(1) == 0)
    def _():
        r_ref[0] = jnp.zeros(r_ref.shape[1:], F32)

    cos, sin = cos_ref[...], sin_ref[...]
    for h in range(n_heads):
        kcols = slice(h * dk, (h + 1) * dk)
        vcols = slice(h * dv, (h + 1) * dv)
        q = _rope(q_ref[0, :, kcols], cos, sin, dk // 2)
        k = _rope(k_ref[0, :, kcols], cos, sin, dk // 2) * scale
        vb = v_ref[0, :, vcols].astype(BF16)
        r = r_ref[0, h]
        sc = lax.dot_general(q.astype(BF16), k.astype(BF16), _NT, preferred_element_type=F32) * dm_ref[h]
        o = (jnp.dot(sc.astype(BF16), vb, preferred_element_type=F32)
             + jnp.dot((q * xi_ref[h]).astype(BF16), r.astype(BF16), preferred_element_type=F32))
        r_ref[0, h] = r * dec_ref[h] + lax.dot_general((k * zt_ref[h]).astype(BF16), vb, _TN,
                                                       preferred_element_type=F32)
        og_ref[0, :, vcols] = _group_norm_gate(o, gn_ref[:, vcols], g_ref[0, :, vcols]).astype(BF16)


def _retention_tables(n_heads, chunk, dk, dv):
    lg = jnp.log(1.0 - 2.0 ** (-5.0 - jnp.arange(n_heads, dtype=F32)))
    idx = jnp.arange(chunk, dtype=F32)
    diff = idx[:, None] - idx[None, :]
    dmask = jnp.where(diff[None] >= 0, jnp.exp(jnp.maximum(diff, 0.0)[None] * lg[:, None, None]), 0.0)
    xi = jnp.exp((idx + 1.0)[None, :] * lg[:, None])
    zeta = jnp.exp((chunk - 1.0 - idx)[None, :] * lg[:, None])
    dec = jnp.exp(chunk * lg)
    return (dmask,
            jnp.broadcast_to(xi[:, :, None], (n_heads, chunk, dk)),
            jnp.broadcast_to(zeta[:, :, None], (n_heads, chunk, dk)),
            jnp.broadcast_to(dec[:, None, None], (n_heads, 1, dv)))


def _rope_tables(pos, half):
    inv = ROPE_BASE ** (-jnp.arange(half, dtype=F32) / half)
    ang = pos[:, None] * inv[None, :]
    return jnp.cos(ang), jnp.sin(ang)


def _ret_prompt(proj, gn_w, n_heads, dk, dv):
    b, l, _ = proj.shape
    d = n_heads * dk
    chunk = math.gcd(l, RET_CHUNK)
    half = dk // 2
    cos, sin = _rope_tables(jnp.arange(l, dtype=F32), half)
    dmask, xiq, ztk, dec = _retention_tables(n_heads, chunk, dk, dv)
    dvs = n_heads * dv
    assert dvs == 2 * d
    const = lambda shape: pl.BlockSpec(shape, lambda bi, c: (0,) * len(shape))
    og, r = pl.pallas_call(
        functools.partial(_ret_prompt_kernel, n_heads=n_heads, dk=dk, dv=dv, scale=dk ** -0.5),
        grid=(b, l // chunk),
        in_specs=[pl.BlockSpec((1, chunk, d), lambda bi, c: (bi, c, 0)),
                  pl.BlockSpec((1, chunk, d), lambda bi, c: (bi, c, 1)),
                  pl.BlockSpec((1, chunk, dvs), lambda bi, c: (bi, c, 1)),
                  pl.BlockSpec((1, chunk, dvs), lambda bi, c: (bi, c, 2)),
                  pl.BlockSpec((chunk, half), lambda bi, c: (c, 0)),
                  pl.BlockSpec((chunk, half), lambda bi, c: (c, 0)),
                  const((n_heads, chunk, chunk)), const((n_heads, chunk, dk)), const((n_heads, chunk, dk)),
                  const((n_heads, 1, dv)), const((1, dvs))],
        out_specs=[pl.BlockSpec((1, chunk, dvs), lambda bi, c: (bi, c, 0)),
                   pl.BlockSpec((1, n_heads, dk, dv), lambda bi, c: (bi, 0, 0, 0))],
        out_shape=[jax.ShapeDtypeStruct((b, l, dvs), BF16),
                   jax.ShapeDtypeStruct((b, n_heads, dk, dv), F32)],
        compiler_params=_params(2),
        name="retention_prompt",
    )(proj, proj, proj, proj, cos, sin, dmask, xiq, ztk, dec, gn_w[None, :])
    return og, r


def _ret_sample_kernel(q_ref, k_ref, v_ref, g_ref, st_ref, cos_ref, sin_ref, xi_ref, dec_ref, gn_ref,
                       og_ref, rn_ref, qpad, kpad, o_sc, *, n_heads, half, scale):
    @pl.when(pl.program_id(0) == 0)
    def _():
        qpad[...] = jnp.zeros(qpad.shape, F32)
        kpad[...] = jnp.zeros(kpad.shape, F32)

    cos, sin = cos_ref[...], sin_ref[...]
    q = _rope(q_ref[0], cos, sin, half)
    k = _rope(k_ref[0], cos, sin, half) * scale
    qpad[0:n_heads, :] = q
    kpad[0:n_heads, :] = k
    q_t = qpad[...].T
    k_t = kpad[...].T
    qk = jnp.sum(q * k, axis=-1, keepdims=True)
    v = v_ref[0]
    for h in range(n_heads):
        rh = st_ref[0, h]
        vh = v[h:h + 1, :]
        qr = jnp.sum(rh * q_t[:, h:h + 1], axis=0, keepdims=True)
        o_sc[h:h + 1, :] = qk[h:h + 1, :] * vh + xi_ref[h:h + 1, :] * qr
        rn_ref[0, h] = rh * dec_ref[h:h + 1, :] + k_t[:, h:h + 1] * vh
    og_ref[0] = _group_norm_gate(o_sc[...], gn_ref[...], g_ref[0]).astype(BF16)


def _ret_sample(proj, state, j, prev_new_state, gn_w, n_heads, dk, dv):
    _, bs, n6 = proj.shape
    half = dk // 2
    cos, sin = _rope_tables(jnp.full((1,), PAST_LEN, F32), half)
    lg = jnp.log(1.0 - 2.0 ** (-5.0 - jnp.arange(n_heads, dtype=F32)))
    gamma = jnp.broadcast_to(jnp.exp(lg)[:, None], (n_heads, dv))
    qk_view = proj.reshape(bs, n6 // dk, dk)
    vg_view = proj.reshape(bs, n6 // dv, dv)
    in_specs = [pl.BlockSpec((1, n_heads, dk), lambda b: (b, 0, 0)),
                pl.BlockSpec((1, n_heads, dk), lambda b: (b, 1, 0)),
                pl.BlockSpec((1, n_heads, dv), lambda b: (b, 1, 0)),
                pl.BlockSpec((1, n_heads, dv), lambda b: (b, 2, 0)),
                pl.BlockSpec((None, 1, n_heads, dk, dv), lambda b: (j, b, 0, 0, 0)),
                pl.BlockSpec((1, half), lambda b: (0, 0)),
                pl.BlockSpec((1, half), lambda b: (0, 0)),
                pl.BlockSpec((n_heads, dv), lambda b: (0, 0)),
                pl.BlockSpec((n_heads, dv), lambda b: (0, 0)),
                pl.BlockSpec((n_heads, dv), lambda b: (0, 0))]
    args = [qk_view, qk_view, vg_view, vg_view, state, cos, sin, gamma, gamma, gn_w.reshape(n_heads, dv)]
    aliases = {}
    kern = functools.partial(_ret_sample_kernel, n_heads=n_heads, half=half, scale=dk ** -0.5)
    if prev_new_state is not None:
        in_specs.append(pl.BlockSpec(memory_space=pl.ANY))
        args.append(prev_new_state)
        aliases = {len(args) - 1: 1}
        inner = kern
        kern = lambda *refs: inner(*refs[:10], *refs[11:])
    og, new_state = pl.pallas_call(
        kern,
        grid=(bs,),
        in_specs=in_specs,
        out_specs=[pl.BlockSpec((1, n_heads, dv), lambda b: (b, 0, 0)),
                   pl.BlockSpec((None, 1, n_heads, dk, dv), lambda b: (j, b, 0, 0, 0))],
        out_shape=[jax.ShapeDtypeStruct((bs, n_heads, dv), BF16),
                   jax.ShapeDtypeStruct(state.shape, state.dtype)],
        scratch_shapes=[pltpu.VMEM((LANES, dk), F32), pltpu.VMEM((LANES, dk), F32),
                        pltpu.VMEM((n_heads, dv), F32)],
        input_output_aliases=aliases,
        compiler_params=_params(1),
        name="retention_sample",
    )(*args)
    return og.reshape(1, bs, n_heads * dv), new_state


def _layer_norm_silu(y, w, b):
    mu = jnp.mean(y, axis=-1, keepdims=True)
    d = y - mu
    var = jnp.mean(d * d, axis=-1, keepdims=True)
    return _silu(d * lax.rsqrt(var + NORM_EPS) * w + b)


def _conv_prompt_kernel(u_ref, wdw_ref, bdw_ref, lnw_ref, lnb_ref, z_ref, nc_ref, ext, *, width, tl, pad):
    li = pl.program_id(1)

    @pl.when(li == 0)
    def _():
        ext[0:pad, :] = jnp.zeros((pad, ext.shape[1]), F32)

    @pl.when(li > 0)
    def _():
        ext[0:pad, :] = ext[tl:tl + pad, :]

    ext[pad:pad + tl, :] = u_ref[0]
    off = pad - (width - 1)
    y = bdw_ref[...] + wdw_ref[0:1, :] * ext[off:off + tl, :]
    for w in range(1, width):
        y = y + wdw_ref[w:w + 1, :] * ext[off + w:off + w + tl, :]
    z_ref[0] = _layer_norm_silu(y, lnw_ref[...], lnb_ref[...]).astype(BF16)

    @pl.when(li == pl.num_programs(1) - 1)
    def _():
        nc_ref[0] = ext[tl + off:tl + pad, :]


def _conv_prompt(u, w_dw, b_dw, ln_w, ln_b, j):
    b, l, d = u.shape
    width = w_dw.shape[1]
    tl = min(l, 128)
    pad = 32
    vec = lambda a: a[:, None, :]
    return pl.pallas_call(
        functools.partial(_conv_prompt_kernel, width=width, tl=tl, pad=pad),
        grid=(b, l // tl),
        in_specs=[pl.BlockSpec((1, tl, d), lambda bi, li: (bi, li, 0)),
                  pl.BlockSpec((None, width, d), lambda bi, li: (j, 0, 0)),
                  pl.BlockSpec((None, 1, d), lambda bi, li: (j, 0, 0)),
                  pl.BlockSpec((None, 1, d), lambda bi, li: (j, 0, 0)),
                  pl.BlockSpec((None, 1, d), lambda bi, li: (j, 0, 0))],
        out_specs=[pl.BlockSpec((1, tl, d), lambda bi, li: (bi, li, 0)),
                   pl.BlockSpec((1, width - 1, d), lambda bi, li: (bi, 0, 0))],
        out_shape=[jax.ShapeDtypeStruct((b, l, d), BF16),
                   jax.ShapeDtypeStruct((b, width - 1, d), F32)],
        scratch_shapes=[pltpu.VMEM((tl + pad, d), F32)],
        compiler_params=_params(2),
        name="conv_prompt",
    )(u, w_dw, vec(b_dw), vec(ln_w), vec(ln_b))


def _conv_sample_kernel(st_ref, u_ref, wdw_ref, bdw_ref, lnw_ref, lnb_ref, z_ref, nc_ref, *, width):
    st = st_ref[...]
    u = u_ref[...]
    y = (jnp.sum(st * wdw_ref[0:width - 1, :], axis=0, keepdims=True)
         + u * wdw_ref[width - 1:width, :] + bdw_ref[...])
    z_ref[...] = _layer_norm_silu(y, lnw_ref[...], lnb_ref[...]).astype(BF16)
    nc_ref[0:width - 2, :] = st[1:width - 1, :]
    nc_ref[width - 2:width - 1, :] = u


def _conv_sample(u, state, w_dw, b_dw, ln_w, ln_b, j):
    _, bs, d = u.shape
    width = w_dw.shape[1]
    vec = lambda a: a[:, None, :]
    z, nc = pl.pallas_call(
        functools.partial(_conv_sample_kernel, width=width),
        grid=(bs,),
        in_specs=[pl.BlockSpec((None, None, width - 1, d), lambda b: (j, b, 0, 0)),
                  pl.BlockSpec((None, 1, d), lambda b: (b, 0, 0)),
                  pl.BlockSpec((None, width, d), lambda b: (j, 0, 0)),
                  pl.BlockSpec((None, 1, d), lambda b: (j, 0, 0)),
                  pl.BlockSpec((None, 1, d), lambda b: (j, 0, 0)),
                  pl.BlockSpec((None, 1, d), lambda b: (j, 0, 0))],
        out_specs=[pl.BlockSpec((None, 1, d), lambda b: (b, 0, 0)),
                   pl.BlockSpec((None, width - 1, d), lambda b: (b, 0, 0))],
        out_shape=[jax.ShapeDtypeStruct((bs, 1, d), BF16),
                   jax.ShapeDtypeStruct((bs, width - 1, d), state.dtype)],
        compiler_params=_params(1),
        name="conv_sample",
    )(state, u.reshape(bs, 1, d), w_dw, vec(b_dw), vec(ln_w), vec(ln_b))
    return z.reshape(1, bs, d), nc


def _top_rows(s, count):
    rows = []
    for _ in range(count):
        m = jnp.max(s, axis=0, keepdims=True)
        rows.append(m)
        s = jnp.where(s == m, -jnp.inf, s)
    return rows


def _stack_rows(rows, base):
    sub = lax.broadcasted_iota(jnp.int32, (8, rows[0].shape[1]), 0)
    out = jnp.broadcast_to(rows[base + 7], sub.shape)
    for r in range(6, -1, -1):
        out = jnp.where(sub == r, rows[base + r], out)
    return out


def _candidates(a, b, topk):
    assert topk == 16
    sub = lax.broadcasted_iota(jnp.int32, (8, a[0].shape[1]), 0)
    b_lo, b_hi, a_hi = _stack_rows(b, 0), _stack_rows(b, 8), _stack_rows(a, 8)
    parts = [a[0] + b_lo, a[0] + b_hi]
    for r1 in range(1, 8):
        n = topk // (r1 + 1)
        c = a[r1] + b_lo
        parts.append(c if n >= 8 else jnp.where(sub < n, c, -jnp.inf))
    parts.append(a_hi + b[0])
    return parts


def _kth_largest(parts, k):
    parts = list(parts)
    for _ in range(k):
        m = parts[0]
        for p in parts[1:]:
            m = jnp.maximum(m, p)
        m = jnp.max(m, axis=0, keepdims=True)
        parts = [jnp.where(p == m, -jnp.inf, p) for p in parts]
    return m


def _route_kernel(h_ref, wq_ref, keys_ref, s1_ref, s2_ref, e1_ref, e2_ref, thr_ref, s_sc, *, half, topk, tr):
    q = jnp.dot(h_ref[0], wq_ref[...].astype(BF16), preferred_element_type=F32).astype(BF16)
    s_sc[0] = lax.dot_general(keys_ref[0, 0].astype(BF16), q[:, :half], _NT, preferred_element_type=F32)
    s_sc[1] = lax.dot_general(keys_ref[0, 1].astype(BF16), q[:, half:], _NT, preferred_element_type=F32)
    for lt in range(tr // LANES):
        lanes = slice(lt * LANES, (lt + 1) * LANES)
        s1 = s_sc[0, :, lanes]
        s2 = s_sc[1, :, lanes]
        a = _top_rows(s1, topk)
        b = _top_rows(s2, topk)
        parts = _candidates(a, b, topk)
        thr = _kth_largest(parts, topk)
        cmax = a[0] + b[0]
        z = jnp.zeros_like(thr)
        for c in parts:
            z = z + jnp.sum(jnp.where(c >= thr, jnp.exp(c - cmax), 0.0), axis=0, keepdims=True)
        s1_ref[0, 0, lt] = s1
        s2_ref[0, 0, lt] = s2
        e1_ref[0, 0, lt] = jnp.exp(s1 - a[0]) * (1.0 / z)
        e2_ref[0, 0, lt] = jnp.exp(s2 - b[0])
        thr_ref[0, 0, lt] = thr


def _route(h, w_q, keys, layer):
    g, r, d = h.shape
    _, hp, _, nk, half = keys.shape
    dq = 2 * half
    tr = min(r, ROUTE_TILE)
    nlt = tr // LANES
    tab = jax.ShapeDtypeStruct((g, hp, r // LANES, nk, LANES), F32)
    tab_spec = pl.BlockSpec((1, 1, nlt, nk, LANES), lambda gi, ti, hi: (gi, hi, ti, 0, 0))
    return pl.pallas_call(
        functools.partial(_route_kernel, half=half, topk=PEER_TOPK, tr=tr),
        grid=(g, r // tr, hp),
        in_specs=[pl.BlockSpec((1, tr, d), lambda gi, ti, hi: (gi, ti, 0)),
                  pl.BlockSpec((None, d, dq), lambda gi, ti, hi: (layer, 0, hi)),
                  pl.BlockSpec((None, 1, 2, nk, half), lambda gi, ti, hi: (layer, hi, 0, 0, 0))],
        out_specs=[tab_spec, tab_spec, tab_spec, tab_spec,
                   pl.BlockSpec((1, 1, nlt, 1, LANES), lambda gi, ti, hi: (gi, hi, ti, 0, 0))],
        out_shape=[tab, tab, tab, tab, jax.ShapeDtypeStruct((g, hp, r // LANES, 1, LANES), F32)],
        scratch_shapes=[pltpu.VMEM((2, nk, tr), F32)],
        compiler_params=_params(3),
        name="peer_route",
    )(h, w_q, keys)


_ERF_NUM = (-2.72614225801306e-10, 2.77068142495902e-08, -2.10102402082508e-06, -5.69250639462346e-05,
            -7.34990630326855e-04, -2.95459980854025e-03, -1.60960333262415e-02)
_ERF_DEN = (-1.45660718464996e-05, -2.13374055278905e-04, -1.68282697438203e-03, -7.37332916720468e-03,
            -1.42647390514189e-02)


_GELU_NUM = tuple(c * 0.5 ** (len(_ERF_NUM) - 1 - m) * 0.5 * 2.0 ** -0.5 for m, c in enumerate(_ERF_NUM))
_GELU_DEN = tuple(c * 0.5 ** (len(_ERF_DEN) - 1 - m) for m, c in enumerate(_ERF_DEN))
_GELU_CLAMP = 4.0 * 2.0 ** 0.5


def _gelu(a):
    ac = jnp.clip(a, -_GELU_CLAMP, _GELU_CLAMP)
    a2 = ac * ac
    p = _GELU_NUM[0] * a2 + _GELU_NUM[1]
    for c in _GELU_NUM[2:]:
        p = p * a2 + c
    q = _GELU_DEN[0] * a2 + _GELU_DEN[1]
    for c in _GELU_DEN[2:]:
        q = q * a2 + c
    return a * (0.5 + ac * p / q)


def _peer_dense_kernel(h_ref, u_ref, v_ref, s1_ref, s2_ref, e1_ref, e2_ref, thr_ref, x_ref, gate_ref,
                       o_ref, a_sc, w_sc, *, n_heads, nk, tt, eb):
    e = pl.program_id(2)

    @pl.when(e == 0)
    def _():
        o_ref[0] = jnp.zeros(o_ref.shape[1:], F32)

    a_sc[...] = lax.dot_general(u_ref[...], h_ref[0], _NT, preferred_element_type=F32)

    rows_per = 16
    tile = (rows_per, LANES)
    for lt in range(tt // LANES):
        lanes = slice(lt * LANES, (lt + 1) * LANES)
        thr_b = [jnp.broadcast_to(thr_ref[0, h, lt], tile) for h in range(n_heads)]
        for ii in range(eb // nk):
            s1_b = [jnp.broadcast_to(s1_ref[0, h, lt, ii:ii + 1, :], tile) for h in range(n_heads)]
            e1_b = [jnp.broadcast_to(e1_ref[0, h, lt, ii:ii + 1, :], tile) for h in range(n_heads)]
            for jc in range(nk // rows_per):
                jrows = slice(jc * rows_per, (jc + 1) * rows_per)
                rows = slice(ii * nk + jc * rows_per, ii * nk + (jc + 1) * rows_per)
                gsum = jnp.zeros(tile, F32)
                for h in range(n_heads):
                    hit = s1_b[h] + s2_ref[0, h, lt, jrows, :] >= thr_b[h]
                    gsum = gsum + jnp.where(hit, e1_b[h] * e2_ref[0, h, lt, jrows, :], 0.0)
                w_sc[rows, lanes] = (gsum * _gelu(a_sc[rows, lanes])).astype(BF16)

    o_ref[0] += lax.dot_general(w_sc[...], v_ref[...], _TN, preferred_element_type=F32)

    @pl.when(e == pl.num_programs(2) - 1)
    def _():
        o_ref[0] = x_ref[0] + gate_ref[0] * o_ref[0]


def _peer_dense(h, u, v, layer, tables, x, mod, gate_chunk):
    g, r, d = h.shape
    s1, s2, e1, e2, thr = tables
    hp, nk = s1.shape[1], s1.shape[3]
    tt = min(r, ROW_TILE)
    eb = EXPERT_TILE
    per = eb // nk
    assert tt % LANES == 0 and per == 8
    nlt = tt // LANES
    full_spec = pl.BlockSpec((1, hp, nlt, nk, LANES), lambda gi, ti, ei: (gi, 0, ti, 0, 0))
    rows_spec = pl.BlockSpec((1, hp, nlt, per, LANES), lambda gi, ti, ei: (gi, 0, ti, ei, 0))
    return pl.pallas_call(
        functools.partial(_peer_dense_kernel, n_heads=hp, nk=nk, tt=tt, eb=eb),
        grid=(g, r // tt, u.shape[1] // eb),
        in_specs=[pl.BlockSpec((1, tt, d), lambda gi, ti, ei: (gi, ti, 0)),
                  pl.BlockSpec((None, eb, d), lambda gi, ti, ei: (layer, ei, 0)),
                  pl.BlockSpec((None, eb, d), lambda gi, ti, ei: (layer, ei, 0)),
                  rows_spec, full_spec, rows_spec, full_spec,
                  pl.BlockSpec((1, hp, nlt, 1, LANES), lambda gi, ti, ei: (gi, 0, ti, 0, 0)),
                  pl.BlockSpec((1, tt, d), lambda gi, ti, ei: (gi, ti, 0)),
                  _mod_spec(mod, tt, gate_chunk, lambda gi, ti, ei: (gi, ti))],
        out_specs=pl.BlockSpec((1, tt, d), lambda gi, ti, ei: (gi, ti, 0)),
        out_shape=jax.ShapeDtypeStruct((g, r, d), F32),
        scratch_shapes=[pltpu.VMEM((eb, tt), F32), pltpu.VMEM((eb, tt), BF16)],
        compiler_params=_params(3),
        name="peer_dense",
    )(h, u, v, s1, s2, e1, e2, thr, x, mod)


def _trunk(x, mods, is_prompt, state_ret, state_conv, norm_w, final_norm_w,
           ret_w_in, ret_gn_w, ret_w_out, conv_w_pw1, conv_b_pw1, conv_w_dw, conv_b_dw,
           conv_ln_w, conv_ln_b, conv_w_pw2, conv_b_pw2, peer_w_q, peer_keys, peer_u, peer_v):
    depth = norm_w.shape[0]
    n_heads, dk, dv = state_ret.shape[2:]
    new_ret, new_conv = None, []
    ret_prompt_states = []
    vec = lambda a: a[:, None, :]
    for i in range(depth):
        mod = mods[i]
        h = _modnorm(x, norm_w[i, 0], mod, 0, 1)
        j = i // 2
        if i % 2 == 0:
            proj = _mm_plain(h, ret_w_in, j)
            if is_prompt:
                og, r = _ret_prompt(proj, ret_gn_w[j], n_heads, dk, dv)
                ret_prompt_states.append(r)
            else:
                og, new_ret = _ret_sample(proj, state_ret, j, new_ret, ret_gn_w[j], n_heads, dk, dv)
            x = _mm_resid(og, ret_w_out, None, j, x, mod, 2)
        else:
            u = _mm_glu(h, conv_w_pw1, vec(conv_b_pw1), j)
            if is_prompt:
                z, nc = _conv_prompt(u, conv_w_dw, conv_b_dw, conv_ln_w, conv_ln_b, j)
            else:
                z, nc = _conv_sample(u, state_conv, conv_w_dw, conv_b_dw, conv_ln_w, conv_ln_b, j)
            new_conv.append(nc)
            x = _mm_resid(z, conv_w_pw2, vec(conv_b_pw2), j, x, mod, 2)
        h = _modnorm(x, norm_w[i, 1], mod, 3, 4)
        tables = _route(h, peer_w_q, peer_keys, i)
        x = _peer_dense(h, peer_u, peer_v, i, tables, x, mod, 5)
    y = _rmsnorm(x, final_norm_w)
    if is_prompt:
        new_ret = jnp.stack(ret_prompt_states)
    return y, new_ret, jnp.stack(new_conv)


def kernel(x_prompt, x_sample, state_ret, state_conv, c_prompt, c_sample, ada_w, ada_b, norm_w, final_norm_w,
           ret_w_in, ret_gn_w, ret_w_out, conv_w_pw1, conv_b_pw1, conv_w_dw, conv_b_dw, conv_ln_w, conv_ln_b,
           conv_w_pw2, conv_b_pw2, peer_w_q, peer_keys, peer_u, peer_v):
    b, l, d = x_prompt.shape
    bs, ls, _ = x_sample.shape
    assert ls == 1, "the sample path handles one new token per sequence"
    depth = ada_w.shape[0]

    n_c = b + bs
    n_pad = -n_c % 8
    c_all = jnp.concatenate([c_prompt, c_sample, jnp.zeros((n_pad, d), c_prompt.dtype)], axis=0)
    mod = _ada(c_all, ada_w, ada_b)
    mods_p = [mod[i, :b].reshape(b, 1, 6 * d) for i in range(depth)]
    mods_s = [mod[i, b:b + bs].reshape(1, bs, 6 * d) for i in range(depth)]

    weights = (norm_w, final_norm_w, ret_w_in, ret_gn_w, ret_w_out, conv_w_pw1, conv_b_pw1, conv_w_dw, conv_b_dw,
               conv_ln_w, conv_ln_b, conv_w_pw2, conv_b_pw2, peer_w_q, peer_keys,
               peer_u.astype(BF16), peer_v.astype(BF16))
    y_p, ret_p, conv_p = _trunk(x_prompt, mods_p, True, state_ret, state_conv, *weights)
    y_s, ret_s, conv_s = _trunk(x_sample.reshape(1, bs, d), mods_s, False, state_ret, state_conv, *weights)
    return (y_p, y_s.reshape(bs, 1, d), ret_p, conv_p, ret_s, conv_s)
```

```python
import functools
import math

import jax
import jax.numpy as jnp
from jax import lax
from jax.experimental import pallas as pl
from jax.experimental.pallas import tpu as pltpu

F32 = jnp.float32
BF16 = jnp.bfloat16

NORM_EPS = 1e-6
ROPE_BASE = 10000.0
RET_CHUNK = 128
PEER_TOPK = 16
PAST_LEN = 16384

LANES = 128
VMEM_LIMIT_BYTES = 56 * 1024 * 1024
ROW_TILE = 512
COL_TILE = 512
EXPERT_TILE = 1024
ROUTE_TILE = 512

_NT = (((1,), (1,)), ((), ()))
_TN = (((0,), (0,)), ((), ()))


def _params(n_axes):
    return pltpu.CompilerParams(dimension_semantics=("arbitrary",) * n_axes,
                                vmem_limit_bytes=VMEM_LIMIT_BYTES)


def _silu(x):
    return x * (1.0 / (1.0 + jnp.exp(-x)))


def _mod_spec(mod, tile, chunk, index_of):
    rm = mod.shape[1]
    d = mod.shape[2] // 6
    if rm == 1:
        return pl.BlockSpec((1, 1, d), lambda *ids: (index_of(*ids)[0], 0, chunk))
    return pl.BlockSpec((1, tile, d), lambda *ids: (index_of(*ids)[0], index_of(*ids)[1], chunk))


def _ada_kernel(c_ref, w_ref, b_ref, o_ref):
    sc = _silu(c_ref[...]).astype(BF16)
    o_ref[...] = jnp.dot(sc, w_ref[...].astype(BF16), preferred_element_type=F32) + b_ref[...]


def _ada(c_all, ada_w, ada_b):
    m, d = c_all.shape
    nl, _, n = ada_w.shape
    tn = min(n, 1024)
    return pl.pallas_call(
        _ada_kernel,
        grid=(nl, n // tn),
        in_specs=[pl.BlockSpec((m, d), lambda l, j: (0, 0)),
                  pl.BlockSpec((None, d, tn), lambda l, j: (l, 0, j)),
                  pl.BlockSpec((None, 1, tn), lambda l, j: (l, 0, j))],
        out_specs=pl.BlockSpec((None, m, tn), lambda l, j: (l, 0, j)),
        out_shape=jax.ShapeDtypeStruct((nl, m, n), F32),
        compiler_params=_params(2),
        name="ada_mod",
    )(c_all, ada_w, ada_b[:, None, :])


def _modnorm_kernel(x_ref, w_ref, sh_ref, sc_ref, o_ref):
    x = x_ref[0]
    y = x * lax.rsqrt(jnp.mean(x * x, axis=-1, keepdims=True) + NORM_EPS) * w_ref[...]
    o_ref[0] = (y * (1.0 + sc_ref[0]) + sh_ref[0]).astype(o_ref.dtype)


def _modnorm(x, w, mod, shift_chunk, scale_chunk):
    g, r, d = x.shape
    tr = min(r, ROW_TILE)
    ix = lambda gi, ti: (gi, ti)
    return pl.pallas_call(
        _modnorm_kernel,
        grid=(g, r // tr),
        in_specs=[pl.BlockSpec((1, tr, d), lambda gi, ti: (gi, ti, 0)),
                  pl.BlockSpec((1, d), lambda gi, ti: (0, 0)),
                  _mod_spec(mod, tr, shift_chunk, ix),
                  _mod_spec(mod, tr, scale_chunk, ix)],
        out_specs=pl.BlockSpec((1, tr, d), lambda gi, ti: (gi, ti, 0)),
        out_shape=jax.ShapeDtypeStruct((g, r, d), BF16),
        compiler_params=_params(2),
        name="modnorm",
    )(x, w[None, :], mod, mod)


def _rmsnorm_kernel(x_ref, w_ref, o_ref):
    x = x_ref[0]
    o_ref[0] = x * lax.rsqrt(jnp.mean(x * x, axis=-1, keepdims=True) + NORM_EPS) * w_ref[...]


def _rmsnorm(x, w):
    g, r, d = x.shape
    tr = min(r, ROW_TILE)
    return pl.pallas_call(
        _rmsnorm_kernel,
        grid=(g, r // tr),
        in_specs=[pl.BlockSpec((1, tr, d), lambda gi, ti: (gi, ti, 0)),
                  pl.BlockSpec((1, d), lambda gi, ti: (0, 0))],
        out_specs=pl.BlockSpec((1, tr, d), lambda gi, ti: (gi, ti, 0)),
        out_shape=jax.ShapeDtypeStruct((g, r, d), F32),
        compiler_params=_params(2),
        name="final_rmsnorm",
    )(x, w[None, :])


def _first_inner_step():
    return jnp.logical_and(pl.program_id(1) == 0, pl.program_id(2) == 0)


def _mm_plain_kernel(x_ref, w_ref, o_ref, wbf):
    @pl.when(_first_inner_step())
    def _():
        wbf[...] = w_ref[...].astype(BF16)
    o_ref[0] = jnp.dot(x_ref[0], wbf[...], preferred_element_type=F32)


def _mm_plain(x, w3, layer):
    g, r, k = x.shape
    n = w3.shape[2]
    tm, tn = min(r, ROW_TILE), min(n, 2 * COL_TILE)
    return pl.pallas_call(
        _mm_plain_kernel,
        grid=(n // tn, g, r // tm),
        in_specs=[pl.BlockSpec((1, tm, k), lambda j, gi, ti: (gi, ti, 0)),
                  pl.BlockSpec((None, k, tn), lambda j, gi, ti: (layer, 0, j))],
        out_specs=pl.BlockSpec((1, tm, tn), lambda j, gi, ti: (gi, ti, j)),
        out_shape=jax.ShapeDtypeStruct((g, r, n), F32),
        scratch_shapes=[pltpu.VMEM((k, tn), BF16)],
        compiler_params=_params(3),
        name="mm_plain",
    )(x, w3)


def _mm_glu_kernel(x_ref, wa_ref, wb_ref, ba_ref, bb_ref, o_ref, wabf, wbbf):
    @pl.when(_first_inner_step())
    def _():
        wabf[...] = wa_ref[...].astype(BF16)
        wbbf[...] = wb_ref[...].astype(BF16)
    x = x_ref[0]
    a = jnp.dot(x, wabf[...], preferred_element_type=F32) + ba_ref[...]
    b = jnp.dot(x, wbbf[...], preferred_element_type=F32) + bb_ref[...]
    o_ref[0] = a * (1.0 / (1.0 + jnp.exp(-b)))


def _mm_glu(x, w3, b3, layer):
    g, r, k = x.shape
    n = w3.shape[2] // 2
    tm, tn = min(r, ROW_TILE), min(n, COL_TILE)
    nb = n // tn
    return pl.pallas_call(
        _mm_glu_kernel,
        grid=(nb, g, r // tm),
        in_specs=[pl.BlockSpec((1, tm, k), lambda j, gi, ti: (gi, ti, 0)),
                  pl.BlockSpec((None, k, tn), lambda j, gi, ti: (layer, 0, j)),
                  pl.BlockSpec((None, k, tn), lambda j, gi, ti: (layer, 0, nb + j)),
                  pl.BlockSpec((None, 1, tn), lambda j, gi, ti: (layer, 0, j)),
                  pl.BlockSpec((None, 1, tn), lambda j, gi, ti: (layer, 0, nb + j))],
        out_specs=pl.BlockSpec((1, tm, tn), lambda j, gi, ti: (gi, ti, j)),
        out_shape=jax.ShapeDtypeStruct((g, r, n), F32),
        scratch_shapes=[pltpu.VMEM((k, tn), BF16), pltpu.VMEM((k, tn), BF16)],
        compiler_params=_params(3),
        name="mm_glu",
    )(x, w3, w3, b3, b3)


def _mm_resid_kernel(x_ref, w_ref, *rest, has_bias):
    if has_bias:
        b_ref, res_ref, gate_ref, o_ref, wbf = rest
    else:
        res_ref, gate_ref, o_ref, wbf = rest

    @pl.when(_first_inner_step())
    def _():
        wbf[...] = w_ref[...].astype(BF16)
    y = jnp.dot(x_ref[0], wbf[...], preferred_element_type=F32)
    if has_bias:
        y = y + b_ref[...]
    o_ref[0] = res_ref[0] + gate_ref[0] * y


def _mm_resid(x, w3, b3, layer, res, mod, gate_chunk):
    g, r, k = x.shape
    n = w3.shape[2]
    tm, tn = min(r, ROW_TILE), min(n, COL_TILE)
    per_d = n // tn
    rm = mod.shape[1]
    if rm == 1:
        gate_spec = pl.BlockSpec((1, 1, tn), lambda j, gi, ti: (gi, 0, gate_chunk * per_d + j))
    else:
        gate_spec = pl.BlockSpec((1, tm, tn), lambda j, gi, ti: (gi, ti, gate_chunk * per_d + j))
    in_specs = [pl.BlockSpec((1, tm, k), lambda j, gi, ti: (gi, ti, 0)),
                pl.BlockSpec((None, k, tn), lambda j, gi, ti: (layer, 0, j))]
    args = [x, w3]
    if b3 is not None:
        in_specs.append(pl.BlockSpec((None, 1, tn), lambda j, gi, ti: (layer, 0, j)))
        args.append(b3)
    in_specs += [pl.BlockSpec((1, tm, tn), lambda j, gi, ti: (gi, ti, j)), gate_spec]
    args += [res, mod]
    return pl.pallas_call(
        functools.partial(_mm_resid_kernel, has_bias=b3 is not None),
        grid=(n // tn, g, r // tm),
        in_specs=in_specs,
        out_specs=pl.BlockSpec((1, tm, tn), lambda j, gi, ti: (gi, ti, j)),
        out_shape=jax.ShapeDtypeStruct((g, r, n), F32),
        scratch_shapes=[pltpu.VMEM((k, tn), BF16)],
        compiler_params=_params(3),
        name="mm_resid",
    )(*args)


def _rope(x, cos, sin, half):
    x1, x2 = x[:, :half], x[:, half:]
    return jnp.concatenate([x1 * cos - x2 * sin, x1 * sin + x2 * cos], axis=-1)


def _group_norm_gate(o, gn_w, g):
    mu = jnp.mean(o, axis=-1, keepdims=True)
    d = o - mu
    var = jnp.mean(d * d, axis=-1, keepdims=True)
    return d * lax.rsqrt(var + NORM_EPS) * gn_w * _silu(g)


def _ret_prompt_kernel(q_ref, k_ref, v_ref, g_ref, cos_ref, sin_ref, dm_ref, xi_ref, zt_ref, dec_ref, gn_ref,
                       og_ref, r_ref, *, n_heads, dk, dv, scale):
    @pl.when(pl.program_id(1) == 0)
    def _():
        r_ref[0] = jnp.zeros(r_ref.shape[1:], F32)

    cos, sin = cos_ref[...], sin_ref[...]
    for h in range(n_heads):
        kcols = slice(h * dk, (h + 1) * dk)
        vcols = slice(h * dv, (h + 1) * dv)
        q = _rope(q_ref[0, :, kcols], cos, sin, dk // 2)
        k = _rope(k_ref[0, :, kcols], cos, sin, dk // 2) * scale
        vb = v_ref[0, :, vcols].astype(BF16)
        r = r_ref[0, h]
        sc = lax.dot_general(q.astype(BF16), k.astype(BF16), _NT, preferred_element_type=F32) * dm_ref[h]
        o = (jnp.dot(sc.astype(BF16), vb, preferred_element_type=F32)
             + jnp.dot((q * xi_ref[h]).astype(BF16), r.astype(BF16), preferred_element_type=F32))
        r_ref[0, h] = r * dec_ref[h] + lax.dot_general((k * zt_ref[h]).astype(BF16), vb, _TN,
                                                       preferred_element_type=F32)
        og_ref[0, :, vcols] = _group_norm_gate(o, gn_ref[:, vcols], g_ref[0, :, vcols]).astype(BF16)


def _retention_tables(n_heads, chunk, dk, dv):
    lg = jnp.log(1.0 - 2.0 ** (-5.0 - jnp.arange(n_heads, dtype=F32)))
    idx = jnp.arange(chunk, dtype=F32)
    diff = idx[:, None] - idx[None, :]
    dmask = jnp.where(diff[None] >= 0, jnp.exp(jnp.maximum(diff, 0.0)[None] * lg[:, None, None]), 0.0)
    xi = jnp.exp((idx + 1.0)[None, :] * lg[:, None])
    zeta = jnp.exp((chunk - 1.0 - idx)[None, :] * lg[:, None])
    dec = jnp.exp(chunk * lg)
    return (dmask,
            jnp.broadcast_to(xi[:, :, None], (n_heads, chunk, dk)),
            jnp.broadcast_to(zeta[:, :, None], (n_heads, chunk, dk)),
            jnp.broadcast_to(dec[:, None, None], (n_heads, 1, dv)))


def _rope_tables(pos, half):
    inv = ROPE_BASE ** (-jnp.arange(half, dtype=F32) / half)
    ang = pos[:, None] * inv[None, :]
    return jnp.cos(ang), jnp.sin(ang)


def _ret_prompt(proj, gn_w, n_heads, dk, dv):
    b, l, _ = proj.shape
    d = n_heads * dk
    chunk = math.gcd(l, RET_CHUNK)
    half = dk // 2
    cos, sin = _rope_tables(jnp.arange(l, dtype=F32), half)
    dmask, xiq, ztk, dec = _retention_tables(n_heads, chunk, dk, dv)
    dvs = n_heads * dv
    assert dvs == 2 * d
    const = lambda shape: pl.BlockSpec(shape, lambda bi, c: (0,) * len(shape))
    og, r = pl.pallas_call(
        functools.partial(_ret_prompt_kernel, n_heads=n_heads, dk=dk, dv=dv, scale=dk ** -0.5),
        grid=(b, l // chunk),
        in_specs=[pl.BlockSpec((1, chunk, d), lambda bi, c: (bi, c, 0)),
                  pl.BlockSpec((1, chunk, d), lambda bi, c: (bi, c, 1)),
                  pl.BlockSpec((1, chunk, dvs), lambda bi, c: (bi, c, 1)),
                  pl.BlockSpec((1, chunk, dvs), lambda bi, c: (bi, c, 2)),
                  pl.BlockSpec((chunk, half), lambda bi, c: (c, 0)),
                  pl.BlockSpec((chunk, half), lambda bi, c: (c, 0)),
                  const((n_heads, chunk, chunk)), const((n_heads, chunk, dk)), const((n_heads, chunk, dk)),
                  const((n_heads, 1, dv)), const((1, dvs))],
        out_specs=[pl.BlockSpec((1, chunk, dvs), lambda bi, c: (bi, c, 0)),
                   pl.BlockSpec((1, n_heads, dk, dv), lambda bi, c: (bi, 0, 0, 0))],
        out_shape=[jax.ShapeDtypeStruct((b, l, dvs), BF16),
                   jax.ShapeDtypeStruct((b, n_heads, dk, dv), F32)],
        compiler_params=_params(2),
        name="retention_prompt",
    )(proj, proj, proj, proj, cos, sin, dmask, xiq, ztk, dec, gn_w[None, :])
    return og, r


def _ret_sample_kernel(q_ref, k_ref, v_ref, g_ref, st_ref, cos_ref, sin_ref, xi_ref, dec_ref, gn_ref,
                       og_ref, rn_ref, qpad, kpad, o_sc, *, n_heads, half, scale):
    @pl.when(pl.program_id(0) == 0)
    def _():
        qpad[...] = jnp.zeros(qpad.shape, F32)
        kpad[...] = jnp.zeros(kpad.shape, F32)

    cos, sin = cos_ref[...], sin_ref[...]
    q = _rope(q_ref[0], cos, sin, half)
    k = _rope(k_ref[0], cos, sin, half) * scale
    qpad[0:n_heads, :] = q
    kpad[0:n_heads, :] = k
    q_t = qpad[...].T
    k_t = kpad[...].T
    qk = jnp.sum(q * k, axis=-1, keepdims=True)
    v = v_ref[0]
    for h in range(n_heads):
        rh = st_ref[0, h]
        vh = v[h:h + 1, :]
        qr = jnp.sum(rh * q_t[:, h:h + 1], axis=0, keepdims=True)
        o_sc[h:h + 1, :] = qk[h:h + 1, :] * vh + xi_ref[h:h + 1, :] * qr
        rn_ref[0, h] = rh * dec_ref[h:h + 1, :] + k_t[:, h:h + 1] * vh
    og_ref[0] = _group_norm_gate(o_sc[...], gn_ref[...], g_ref[0]).astype(BF16)


def _ret_sample(proj, state, j, prev_new_state, gn_w, n_heads, dk, dv):
    _, bs, n6 = proj.shape
    half = dk // 2
    cos, sin = _rope_tables(jnp.full((1,), PAST_LEN, F32), half)
    lg = jnp.log(1.0 - 2.0 ** (-5.0 - jnp.arange(n_heads, dtype=F32)))
    gamma = jnp.broadcast_to(jnp.exp(lg)[:, None], (n_heads, dv))
    qk_view = proj.reshape(bs, n6 // dk, dk)
    vg_view = proj.reshape(bs, n6 // dv, dv)
    in_specs = [pl.BlockSpec((1, n_heads, dk), lambda b: (b, 0, 0)),
                pl.BlockSpec((1, n_heads, dk), lambda b: (b, 1, 0)),
                pl.BlockSpec((1, n_heads, dv), lambda b: (b, 1, 0)),
                pl.BlockSpec((1, n_heads, dv), lambda b: (b, 2, 0)),
                pl.BlockSpec((None, 1, n_heads, dk, dv), lambda b: (j, b, 0, 0, 0)),
                pl.BlockSpec((1, half), lambda b: (0, 0)),
                pl.BlockSpec((1, half), lambda b: (0, 0)),
                pl.BlockSpec((n_heads, dv), lambda b: (0, 0)),
                pl.BlockSpec((n_heads, dv), lambda b: (0, 0)),
                pl.BlockSpec((n_heads, dv), lambda b: (0, 0))]
    args = [qk_view, qk_view, vg_view, vg_view, state, cos, sin, gamma, gamma, gn_w.reshape(n_heads, dv)]
    aliases = {}
    kern = functools.partial(_ret_sample_kernel, n_heads=n_heads, half=half, scale=dk ** -0.5)
    if prev_new_state is not None:
        in_specs.append(pl.BlockSpec(memory_space=pl.ANY))
        args.append(prev_new_state)
        aliases = {len(args) - 1: 1}
        inner = kern
        kern = lambda *refs: inner(*refs[:10], *refs[11:])
    og, new_state = pl.pallas_call(
        kern,
        grid=(bs,),
        in_specs=in_specs,
        out_specs=[pl.BlockSpec((1, n_heads, dv), lambda b: (b, 0, 0)),
                   pl.BlockSpec((None, 1, n_heads, dk, dv), lambda b: (j, b, 0, 0, 0))],
        out_shape=[jax.ShapeDtypeStruct((bs, n_heads, dv), BF16),
                   jax.ShapeDtypeStruct(state.shape, state.dtype)],
        scratch_shapes=[pltpu.VMEM((LANES, dk), F32), pltpu.VMEM((LANES, dk), F32),
                        pltpu.VMEM((n_heads, dv), F32)],
        input_output_aliases=aliases,
        compiler_params=_params(1),
        name="retention_sample",
    )(*args)
    return og.reshape(1, bs, n_heads * dv), new_state


def _layer_norm_silu(y, w, b):
    mu = jnp.mean(y, axis=-1, keepdims=True)
    d = y - mu
    var = jnp.mean(d * d, axis=-1, keepdims=True)
    return _silu(d * lax.rsqrt(var + NORM_EPS) * w + b)


def _conv_prompt_kernel(u_ref, wdw_ref, bdw_ref, lnw_ref, lnb_ref, z_ref, nc_ref, ext, zs, ys, *, width, tl, pad):
    li = pl.program_id(1)

    @pl.when(li == 0)
    def _():
        ext[0:pad, :] = jnp.zeros((pad, ext.shape[1]), F32)

    @pl.when(li > 0)
    def _():
        ext[0:pad, :] = ext[tl:tl + pad, :]

    ext[pad:pad + tl, :] = u_ref[0]
    off = pad - (width - 1)
    for s0 in range(0, ext.shape[1], LANES):
        cols = slice(s0, s0 + LANES)
        y = None
        for b in range(8):
            rows = tl if b == 0 else tl + 8
            acc = None
            for c in range(off, off + width):
                if c % 8 == b:
                    term = wdw_ref[c - off:c - off + 1, cols] * ext[c - b:c - b + rows, cols]
                    acc = term if acc is None else acc + term
            if b == 0:
                y = bdw_ref[:, cols] + acc
            else:
                zs[0:rows, cols] = acc
                y = y + zs[b:b + tl, cols]
        ys[:, cols] = y
    z_ref[0] = _layer_norm_silu(ys[...], lnw_ref[...], lnb_ref[...]).astype(BF16)

    @pl.when(li == pl.num_programs(1) - 1)
    def _():
        nc_ref[0] = ext[tl + off:tl + pad, :]


def _conv_prompt(u, w_dw, b_dw, ln_w, ln_b, j):
    b, l, d = u.shape
    width = w_dw.shape[1]
    tl = min(l, 128)
    pad = 32
    vec = lambda a: a[:, None, :]
    return pl.pallas_call(
        functools.partial(_conv_prompt_kernel, width=width, tl=tl, pad=pad),
        grid=(b, l // tl),
        in_specs=[pl.BlockSpec((1, tl, d), lambda bi, li: (bi, li, 0)),
                  pl.BlockSpec((None, width, d), lambda bi, li: (j, 0, 0)),
                  pl.BlockSpec((None, 1, d), lambda bi, li: (j, 0, 0)),
                  pl.BlockSpec((None, 1, d), lambda bi, li: (j, 0, 0)),
                  pl.BlockSpec((None, 1, d), lambda bi, li: (j, 0, 0))],
        out_specs=[pl.BlockSpec((1, tl, d), lambda bi, li: (bi, li, 0)),
                   pl.BlockSpec((1, width - 1, d), lambda bi, li: (bi, 0, 0))],
        out_shape=[jax.ShapeDtypeStruct((b, l, d), BF16),
                   jax.ShapeDtypeStruct((b, width - 1, d), F32)],
        scratch_shapes=[pltpu.VMEM((tl + pad, d), F32), pltpu.VMEM((tl + 8, d), F32), pltpu.VMEM((tl, d), F32)],
        compiler_params=_params(2),
        name="conv_prompt",
    )(u, w_dw, vec(b_dw), vec(ln_w), vec(ln_b))


def _conv_sample_kernel(st_ref, u_ref, wdw_ref, bdw_ref, lnw_ref, lnb_ref, z_ref, nc_ref, *, width):
    st = st_ref[...]
    u = u_ref[...]
    y = (jnp.sum(st * wdw_ref[0:width - 1, :], axis=0, keepdims=True)
         + u * wdw_ref[width - 1:width, :] + bdw_ref[...])
    z_ref[...] = _layer_norm_silu(y, lnw_ref[...], lnb_ref[...]).astype(BF16)
    nc_ref[0:width - 2, :] = st[1:width - 1, :]
    nc_ref[width - 2:width - 1, :] = u


def _conv_sample(u, state, w_dw, b_dw, ln_w, ln_b, j):
    _, bs, d = u.shape
    width = w_dw.shape[1]
    vec = lambda a: a[:, None, :]
    z, nc = pl.pallas_call(
        functools.partial(_conv_sample_kernel, width=width),
        grid=(bs,),
        in_specs=[pl.BlockSpec((None, None, width - 1, d), lambda b: (j, b, 0, 0)),
                  pl.BlockSpec((None, 1, d), lambda b: (b, 0, 0)),
                  pl.BlockSpec((None, width, d), lambda b: (j, 0, 0)),
                  pl.BlockSpec((None, 1, d), lambda b: (j, 0, 0)),
                  pl.BlockSpec((None, 1, d), lambda b: (j, 0, 0)),
                  pl.BlockSpec((None, 1, d), lambda b: (j, 0, 0))],
        out_specs=[pl.BlockSpec((None, 1, d), lambda b: (b, 0, 0)),
                   pl.BlockSpec((None, width - 1, d), lambda b: (b, 0, 0))],
        out_shape=[jax.ShapeDtypeStruct((bs, 1, d), BF16),
                   jax.ShapeDtypeStruct((bs, width - 1, d), state.dtype)],
        compiler_params=_params(1),
        name="conv_sample",
    )(state, u.reshape(bs, 1, d), w_dw, vec(b_dw), vec(ln_w), vec(ln_b))
    return z.reshape(1, bs, d), nc


def _top_rows(s, count):
    rows = []
    for _ in range(count):
        m = jnp.max(s, axis=0, keepdims=True)
        rows.append(m)
        s = jnp.where(s == m, -jnp.inf, s)
    return rows


def _stack_rows(rows, base):
    sub = lax.broadcasted_iota(jnp.int32, (8, rows[0].shape[1]), 0)
    out = jnp.broadcast_to(rows[base + 7], sub.shape)
    for r in range(6, -1, -1):
        out = jnp.where(sub == r, rows[base + r], out)
    return out


def _candidates(a, b, topk):
    assert topk == 16
    sub = lax.broadcasted_iota(jnp.int32, (8, a[0].shape[1]), 0)
    b_lo, b_hi, a_hi = _stack_rows(b, 0), _stack_rows(b, 8), _stack_rows(a, 8)
    parts = [a[0] + b_lo, a[0] + b_hi]
    for r1 in range(1, 8):
        n = topk // (r1 + 1)
        c = a[r1] + b_lo
        parts.append(c if n >= 8 else jnp.where(sub < n, c, -jnp.inf))
    parts.append(a_hi + b[0])
    return parts


def _kth_largest(parts, k):
    parts = list(parts)
    for _ in range(k):
        m = parts[0]
        for p in parts[1:]:
            m = jnp.maximum(m, p)
        m = jnp.max(m, axis=0, keepdims=True)
        parts = [jnp.where(p == m, -jnp.inf, p) for p in parts]
    return m


def _route_kernel(h_ref, wq_ref, keys_ref, s1_ref, s2_ref, e1_ref, e2_ref, thr_ref, s_sc, *, half, topk, tr):
    q = jnp.dot(h_ref[0], wq_ref[...].astype(BF16), preferred_element_type=F32).astype(BF16)
    s_sc[0] = lax.dot_general(keys_ref[0, 0].astype(BF16), q[:, :half], _NT, preferred_element_type=F32)
    s_sc[1] = lax.dot_general(keys_ref[0, 1].astype(BF16), q[:, half:], _NT, preferred_element_type=F32)
    for lt in range(tr // LANES):
        lanes = slice(lt * LANES, (lt + 1) * LANES)
        s1 = s_sc[0, :, lanes]
        s2 = s_sc[1, :, lanes]
        a = _top_rows(s1, topk)
        b = _top_rows(s2, topk)
        parts = _candidates(a, b, topk)
        thr = _kth_largest(parts, topk)
        cmax = a[0] + b[0]
        z = jnp.zeros_like(thr)
        for c in parts:
            z = z + jnp.sum(jnp.where(c >= thr, jnp.exp(c - cmax), 0.0), axis=0, keepdims=True)
        s1_ref[0, 0, lt] = s1
        s2_ref[0, 0, lt] = s2
        e1_ref[0, 0, lt] = jnp.exp(s1 - a[0]) * (1.0 / z)
        e2_ref[0, 0, lt] = jnp.exp(s2 - b[0]).astype(BF16)
        thr_ref[0, 0, lt] = thr


def _route(h, w_q, keys, layer):
    g, r, d = h.shape
    _, hp, _, nk, half = keys.shape
    dq = 2 * half
    tr = min(r, ROUTE_TILE)
    nlt = tr // LANES
    tab = jax.ShapeDtypeStruct((g, hp, r // LANES, nk, LANES), F32)
    tab_spec = pl.BlockSpec((1, 1, nlt, nk, LANES), lambda gi, ti, hi: (gi, hi, ti, 0, 0))
    return pl.pallas_call(
        functools.partial(_route_kernel, half=half, topk=PEER_TOPK, tr=tr),
        grid=(g, r // tr, hp),
        in_specs=[pl.BlockSpec((1, tr, d), lambda gi, ti, hi: (gi, ti, 0)),
                  pl.BlockSpec((None, d, dq), lambda gi, ti, hi: (layer, 0, hi)),
                  pl.BlockSpec((None, 1, 2, nk, half), lambda gi, ti, hi: (layer, hi, 0, 0, 0))],
        out_specs=[tab_spec, tab_spec, tab_spec, tab_spec,
                   pl.BlockSpec((1, 1, nlt, 1, LANES), lambda gi, ti, hi: (gi, hi, ti, 0, 0))],
        out_shape=[tab, tab, tab, jax.ShapeDtypeStruct(tab.shape, BF16),
                   jax.ShapeDtypeStruct((g, hp, r // LANES, 1, LANES), F32)],
        scratch_shapes=[pltpu.VMEM((2, nk, tr), F32)],
        compiler_params=_params(3),
        name="peer_route",
    )(h, w_q, keys)


_ERF_NUM = (-2.72614225801306e-10, 2.77068142495902e-08, -2.10102402082508e-06, -5.69250639462346e-05,
            -7.34990630326855e-04, -2.95459980854025e-03, -1.60960333262415e-02)
_ERF_DEN = (-1.45660718464996e-05, -2.13374055278905e-04, -1.68282697438203e-03, -7.37332916720468e-03,
            -1.42647390514189e-02)


_GELU_NUM = tuple(c * 0.5 ** (len(_ERF_NUM) - 1 - m) * 0.5 * 2.0 ** -0.5 for m, c in enumerate(_ERF_NUM))
_GELU_DEN = tuple(c * 0.5 ** (len(_ERF_DEN) - 1 - m) for m, c in enumerate(_ERF_DEN))
_GELU_CLAMP = 4.0 * 2.0 ** 0.5


def _gelu(a):
    ac = jnp.clip(a, -_GELU_CLAMP, _GELU_CLAMP)
    a2 = ac * ac
    p = _GELU_NUM[0] * a2 + _GELU_NUM[1]
    for c in _GELU_NUM[2:]:
        p = p * a2 + c
    q = _GELU_DEN[0] * a2 + _GELU_DEN[1]
    for c in _GELU_DEN[2:]:
        q = q * a2 + c
    return a * (0.5 + ac * p / q)


def _peer_dense_kernel(h_ref, u_ref, v_ref, s1_ref, s2_ref, e1_ref, e2_ref, thr_ref, x_ref, gate_ref,
                       o_ref, a_sc, w_sc, *, n_heads, nk, tt, eb):
    e = pl.program_id(2)

    @pl.when(e == 0)
    def _():
        o_ref[0] = jnp.zeros(o_ref.shape[1:], F32)

    a_sc[...] = lax.dot_general(u_ref[...], h_ref[0], _NT, preferred_element_type=F32)

    rows_per = 16
    tile = (rows_per, LANES)
    for lt in range(tt // LANES):
        lanes = slice(lt * LANES, (lt + 1) * LANES)
        thr_b = [jnp.broadcast_to(thr_ref[0, h, lt], tile) for h in range(n_heads)]
        for ii in range(eb // nk):
            s1_b = [jnp.broadcast_to(s1_ref[0, h, lt, ii:ii + 1, :], tile) for h in range(n_heads)]
            e1_b = [jnp.broadcast_to(e1_ref[0, h, lt, ii:ii + 1, :], tile).astype(BF16) for h in range(n_heads)]
            for jc in range(nk // rows_per):
                jrows = slice(jc * rows_per, (jc + 1) * rows_per)
                rows = slice(ii * nk + jc * rows_per, ii * nk + (jc + 1) * rows_per)
                gsum = jnp.zeros(tile, BF16)
                for h in range(n_heads):
                    hit = s1_b[h] + s2_ref[0, h, lt, jrows, :] >= thr_b[h]
                    gsum = gsum + jnp.where(hit, e1_b[h] * e2_ref[0, h, lt, jrows, :], jnp.zeros(tile, BF16))
                w_sc[rows, lanes] = (gsum.astype(F32) * _gelu(a_sc[rows, lanes])).astype(BF16)

    o_ref[0] += lax.dot_general(w_sc[...], v_ref[...], _TN, preferred_element_type=F32)

    @pl.when(e == pl.num_programs(2) - 1)
    def _():
        o_ref[0] = x_ref[0] + gate_ref[0] * o_ref[0]


def _peer_dense(h, u, v, layer, tables, x, mod, gate_chunk):
    g, r, d = h.shape
    s1, s2, e1, e2, thr = tables
    hp, nk = s1.shape[1], s1.shape[3]
    tt = min(r, ROW_TILE)
    eb = EXPERT_TILE
    per = eb // nk
    assert tt % LANES == 0 and per == 8
    nlt = tt // LANES
    full_spec = pl.BlockSpec((1, hp, nlt, nk, LANES), lambda gi, ti, ei: (gi, 0, ti, 0, 0))
    rows_spec = pl.BlockSpec((1, hp, nlt, per, LANES), lambda gi, ti, ei: (gi, 0, ti, ei, 0))
    return pl.pallas_call(
        functools.partial(_peer_dense_kernel, n_heads=hp, nk=nk, tt=tt, eb=eb),
        grid=(g, r // tt, u.shape[1] // eb),
        in_specs=[pl.BlockSpec((1, tt, d), lambda gi, ti, ei: (gi, ti, 0)),
                  pl.BlockSpec((None, eb, d), lambda gi, ti, ei: (layer, ei, 0)),
                  pl.BlockSpec((None, eb, d), lambda gi, ti, ei: (layer, ei, 0)),
                  rows_spec, full_spec, rows_spec, full_spec,
                  pl.BlockSpec((1, hp, nlt, 1, LANES), lambda gi, ti, ei: (gi, 0, ti, 0, 0)),
                  pl.BlockSpec((1, tt, d), lambda gi, ti, ei: (gi, ti, 0)),
                  _mod_spec(mod, tt, gate_chunk, lambda gi, ti, ei: (gi, ti))],
        out_specs=pl.BlockSpec((1, tt, d), lambda gi, ti, ei: (gi, ti, 0)),
        out_shape=jax.ShapeDtypeStruct((g, r, d), F32),
        scratch_shapes=[pltpu.VMEM((eb, tt), F32), pltpu.VMEM((eb, tt), BF16)],
        compiler_params=_params(3),
        name="peer_dense",
    )(h, u, v, s1, s2, e1, e2, thr, x, mod)


def _trunk(x, mods, is_prompt, state_ret, state_conv, norm_w, final_norm_w,
           ret_w_in, ret_gn_w, ret_w_out, conv_w_pw1, conv_b_pw1, conv_w_dw, conv_b_dw,
           conv_ln_w, conv_ln_b, conv_w_pw2, conv_b_pw2, peer_w_q, peer_keys, peer_u, peer_v):
    depth = norm_w.shape[0]
    n_heads, dk, dv = state_ret.shape[2:]
    new_ret, new_conv = None, []
    ret_prompt_states = []
    vec = lambda a: a[:, None, :]
    for i in range(depth):
        mod = mods[i]
        h = _modnorm(x, norm_w[i, 0], mod, 0, 1)
        j = i // 2
        if i % 2 == 0:
            proj = _mm_plain(h, ret_w_in, j)
            if is_prompt:
                og, r = _ret_prompt(proj, ret_gn_w[j], n_heads, dk, dv)
                ret_prompt_states.append(r)
            else:
                og, new_ret = _ret_sample(proj, state_ret, j, new_ret, ret_gn_w[j], n_heads, dk, dv)
            x = _mm_resid(og, ret_w_out, None, j, x, mod, 2)
        else:
            u = _mm_glu(h, conv_w_pw1, vec(conv_b_pw1), j)
            if is_prompt:
                z, nc = _conv_prompt(u, conv_w_dw, conv_b_dw, conv_ln_w, conv_ln_b, j)
            else:
                z, nc = _conv_sample(u, state_conv, conv_w_dw, conv_b_dw, conv_ln_w, conv_ln_b, j)
            new_conv.append(nc)
            x = _mm_resid(z, conv_w_pw2, vec(conv_b_pw2), j, x, mod, 2)
        h = _modnorm(x, norm_w[i, 1], mod, 3, 4)
        tables = _route(h, peer_w_q, peer_keys, i)
        x = _peer_dense(h, peer_u, peer_v, i, tables, x, mod, 5)
    y = _rmsnorm(x, final_norm_w)
    if is_prompt:
        new_ret = jnp.stack(ret_prompt_states)
    return y, new_ret, jnp.stack(new_conv)


def kernel(x_prompt, x_sample, state_ret, state_conv, c_prompt, c_sample, ada_w, ada_b, norm_w, final_norm_w,
           ret_w_in, ret_gn_w, ret_w_out, conv_w_pw1, conv_b_pw1, conv_w_dw, conv_b_dw, conv_ln_w, conv_ln_b,
           conv_w_pw2, conv_b_pw2, peer_w_q, peer_keys, peer_u, peer_v):
    b, l, d = x_prompt.shape
    bs, ls, _ = x_sample.shape
    assert ls == 1, "the sample path handles one new token per sequence"
    depth = ada_w.shape[0]

    n_c = b + bs
    n_pad = -n_c % 8
    c_all = jnp.concatenate([c_prompt, c_sample, jnp.zeros((n_pad, d), c_prompt.dtype)], axis=0)
    mod = _ada(c_all, ada_w, ada_b)
    mods_p = [mod[i, :b].reshape(b, 1, 6 * d) for i in range(depth)]
    mods_s = [mod[i, b:b + bs].reshape(1, bs, 6 * d) for i in range(depth)]

    weights = (norm_w, final_norm_w, ret_w_in, ret_gn_w, ret_w_out, conv_w_pw1, conv_b_pw1, conv_w_dw, conv_b_dw,
               conv_ln_w, conv_ln_b, conv_w_pw2, conv_b_pw2, peer_w_q, peer_keys,
               peer_u.astype(BF16), peer_v.astype(BF16))
    y_p, ret_p, conv_p = _trunk(x_prompt, mods_p, True, state_ret, state_conv, *weights)
    y_s, ret_s, conv_s = _trunk(x_sample.reshape(1, bs, d), mods_s, False, state_ret, state_conv, *weights)
    return (y_p, y_s.reshape(bs, 1, d), ret_p, conv_p, ret_s, conv_s)
```

```python
import functools
import math

import jax
import jax.numpy as jnp
import numpy as np
from jax import lax
from jax.experimental import pallas as pl
from jax.experimental.pallas import tpu as pltpu

F32 = jnp.float32
BF16 = jnp.bfloat16

NORM_EPS = 1e-6
ROPE_BASE = 10000.0
RET_CHUNK = 128
PEER_TOPK = 16
PAST_LEN = 16384

LANES = 128
VMEM_LIMIT_BYTES = 56 * 1024 * 1024
ROW_TILE = 512
COL_TILE = 512
EXPERT_TILE = 1024
ROUTE_TILE = 512

_NT = (((1,), (1,)), ((), ()))
_TN = (((0,), (0,)), ((), ()))


def _params(n_axes):
    return pltpu.CompilerParams(dimension_semantics=("arbitrary",) * n_axes,
                                vmem_limit_bytes=VMEM_LIMIT_BYTES)


def _silu(x):
    return x * (1.0 / (1.0 + jnp.exp(-x)))


def _mod_spec(mod, tile, chunk, index_of):
    rm = mod.shape[1]
    d = mod.shape[2] // 6
    if rm == 1:
        return pl.BlockSpec((1, 1, d), lambda *ids: (index_of(*ids)[0], 0, chunk))
    return pl.BlockSpec((1, tile, d), lambda *ids: (index_of(*ids)[0], index_of(*ids)[1], chunk))


def _ada_kernel(c_ref, w_ref, b_ref, o_ref):
    sc = _silu(c_ref[...]).astype(BF16)
    o_ref[...] = jnp.dot(sc, w_ref[...].astype(BF16), preferred_element_type=F32) + b_ref[...]


def _ada(c_all, ada_w, ada_b):
    m, d = c_all.shape
    nl, _, n = ada_w.shape
    tn = min(n, 1024)
    return pl.pallas_call(
        _ada_kernel,
        grid=(nl, n // tn),
        in_specs=[pl.BlockSpec((m, d), lambda l, j: (0, 0)),
                  pl.BlockSpec((None, d, tn), lambda l, j: (l, 0, j)),
                  pl.BlockSpec((None, 1, tn), lambda l, j: (l, 0, j))],
        out_specs=pl.BlockSpec((None, m, tn), lambda l, j: (l, 0, j)),
        out_shape=jax.ShapeDtypeStruct((nl, m, n), F32),
        compiler_params=_params(2),
        name="ada_mod",
    )(c_all, ada_w, ada_b[:, None, :])


def _modnorm_kernel(x_ref, w_ref, sh_ref, sc_ref, o_ref):
    x = x_ref[0]
    y = x * lax.rsqrt(jnp.mean(x * x, axis=-1, keepdims=True) + NORM_EPS) * w_ref[...]
    o_ref[0] = (y * (1.0 + sc_ref[0]) + sh_ref[0]).astype(o_ref.dtype)


def _modnorm(x, w, mod, shift_chunk, scale_chunk):
    g, r, d = x.shape
    tr = min(r, ROW_TILE)
    ix = lambda gi, ti: (gi, ti)
    return pl.pallas_call(
        _modnorm_kernel,
        grid=(g, r // tr),
        in_specs=[pl.BlockSpec((1, tr, d), lambda gi, ti: (gi, ti, 0)),
                  pl.BlockSpec((1, d), lambda gi, ti: (0, 0)),
                  _mod_spec(mod, tr, shift_chunk, ix),
                  _mod_spec(mod, tr, scale_chunk, ix)],
        out_specs=pl.BlockSpec((1, tr, d), lambda gi, ti: (gi, ti, 0)),
        out_shape=jax.ShapeDtypeStruct((g, r, d), BF16),
        compiler_params=_params(2),
        name="modnorm",
    )(x, w[None, :], mod, mod)


def _rmsnorm_kernel(x_ref, w_ref, o_ref):
    x = x_ref[0]
    o_ref[0] = x * lax.rsqrt(jnp.mean(x * x, axis=-1, keepdims=True) + NORM_EPS) * w_ref[...]


def _rmsnorm(x, w):
    g, r, d = x.shape
    tr = min(r, ROW_TILE)
    return pl.pallas_call(
        _rmsnorm_kernel,
        grid=(g, r // tr),
        in_specs=[pl.BlockSpec((1, tr, d), lambda gi, ti: (gi, ti, 0)),
                  pl.BlockSpec((1, d), lambda gi, ti: (0, 0))],
        out_specs=pl.BlockSpec((1, tr, d), lambda gi, ti: (gi, ti, 0)),
        out_shape=jax.ShapeDtypeStruct((g, r, d), F32),
        compiler_params=_params(2),
        name="final_rmsnorm",
    )(x, w[None, :])


def _first_inner_step():
    return jnp.logical_and(pl.program_id(1) == 0, pl.program_id(2) == 0)


def _mm_plain_kernel(x_ref, w_ref, o_ref, wbf):
    @pl.when(_first_inner_step())
    def _():
        wbf[...] = w_ref[...].astype(BF16)
    o_ref[0] = jnp.dot(x_ref[0], wbf[...], preferred_element_type=F32)


def _mm_plain(x, w3, layer):
    g, r, k = x.shape
    n = w3.shape[2]
    tm, tn = min(r, ROW_TILE), min(n, 2 * COL_TILE)
    return pl.pallas_call(
        _mm_plain_kernel,
        grid=(n // tn, g, r // tm),
        in_specs=[pl.BlockSpec((1, tm, k), lambda j, gi, ti: (gi, ti, 0)),
                  pl.BlockSpec((None, k, tn), lambda j, gi, ti: (layer, 0, j))],
        out_specs=pl.BlockSpec((1, tm, tn), lambda j, gi, ti: (gi, ti, j)),
        out_shape=jax.ShapeDtypeStruct((g, r, n), F32),
        scratch_shapes=[pltpu.VMEM((k, tn), BF16)],
        compiler_params=_params(3),
        name="mm_plain",
    )(x, w3)


def _mm_glu_kernel(x_ref, wa_ref, wb_ref, ba_ref, bb_ref, o_ref, wabf, wbbf):
    @pl.when(_first_inner_step())
    def _():
        wabf[...] = wa_ref[...].astype(BF16)
        wbbf[...] = wb_ref[...].astype(BF16)
    x = x_ref[0]
    a = jnp.dot(x, wabf[...], preferred_element_type=F32) + ba_ref[...]
    b = jnp.dot(x, wbbf[...], preferred_element_type=F32) + bb_ref[...]
    o_ref[0] = a * (1.0 / (1.0 + jnp.exp(-b)))


def _mm_glu(x, w3, b3, layer):
    g, r, k = x.shape
    n = w3.shape[2] // 2
    tm, tn = min(r, ROW_TILE), min(n, COL_TILE)
    nb = n // tn
    return pl.pallas_call(
        _mm_glu_kernel,
        grid=(nb, g, r // tm),
        in_specs=[pl.BlockSpec((1, tm, k), lambda j, gi, ti: (gi, ti, 0)),
                  pl.BlockSpec((None, k, tn), lambda j, gi, ti: (layer, 0, j)),
                  pl.BlockSpec((None, k, tn), lambda j, gi, ti: (layer, 0, nb + j)),
                  pl.BlockSpec((None, 1, tn), lambda j, gi, ti: (layer, 0, j)),
                  pl.BlockSpec((None, 1, tn), lambda j, gi, ti: (layer, 0, nb + j))],
        out_specs=pl.BlockSpec((1, tm, tn), lambda j, gi, ti: (gi, ti, j)),
        out_shape=jax.ShapeDtypeStruct((g, r, n), F32),
        scratch_shapes=[pltpu.VMEM((k, tn), BF16), pltpu.VMEM((k, tn), BF16)],
        compiler_params=_params(3),
        name="mm_glu",
    )(x, w3, w3, b3, b3)


def _mm_resid_kernel(x_ref, w_ref, *rest, has_bias):
    if has_bias:
        b_ref, res_ref, gate_ref, o_ref, wbf = rest
    else:
        res_ref, gate_ref, o_ref, wbf = rest

    @pl.when(_first_inner_step())
    def _():
        wbf[...] = w_ref[...].astype(BF16)
    y = jnp.dot(x_ref[0], wbf[...], preferred_element_type=F32)
    if has_bias:
        y = y + b_ref[...]
    o_ref[0] = res_ref[0] + gate_ref[0] * y


def _mm_resid(x, w3, b3, layer, res, mod, gate_chunk):
    g, r, k = x.shape
    n = w3.shape[2]
    tm, tn = min(r, ROW_TILE), min(n, COL_TILE)
    per_d = n // tn
    rm = mod.shape[1]
    if rm == 1:
        gate_spec = pl.BlockSpec((1, 1, tn), lambda j, gi, ti: (gi, 0, gate_chunk * per_d + j))
    else:
        gate_spec = pl.BlockSpec((1, tm, tn), lambda j, gi, ti: (gi, ti, gate_chunk * per_d + j))
    in_specs = [pl.BlockSpec((1, tm, k), lambda j, gi, ti: (gi, ti, 0)),
                pl.BlockSpec((None, k, tn), lambda j, gi, ti: (layer, 0, j))]
    args = [x, w3]
    if b3 is not None:
        in_specs.append(pl.BlockSpec((None, 1, tn), lambda j, gi, ti: (layer, 0, j)))
        args.append(b3)
    in_specs += [pl.BlockSpec((1, tm, tn), lambda j, gi, ti: (gi, ti, j)), gate_spec]
    args += [res, mod]
    return pl.pallas_call(
        functools.partial(_mm_resid_kernel, has_bias=b3 is not None),
        grid=(n // tn, g, r // tm),
        in_specs=in_specs,
        out_specs=pl.BlockSpec((1, tm, tn), lambda j, gi, ti: (gi, ti, j)),
        out_shape=jax.ShapeDtypeStruct((g, r, n), F32),
        scratch_shapes=[pltpu.VMEM((k, tn), BF16)],
        compiler_params=_params(3),
        name="mm_resid",
    )(*args)


def _rope(x, cos, sin, half):
    x1, x2 = x[:, :half], x[:, half:]
    return jnp.concatenate([x1 * cos - x2 * sin, x1 * sin + x2 * cos], axis=-1)


def _group_norm_gate(o, gn_w, g):
    mu = jnp.mean(o, axis=-1, keepdims=True)
    d = o - mu
    var = jnp.mean(d * d, axis=-1, keepdims=True)
    return d * lax.rsqrt(var + NORM_EPS) * gn_w * _silu(g)


def _ret_prompt_kernel(q_ref, k_ref, v_ref, g_ref, cos_ref, sin_ref, dm_ref, xi_ref, zt_ref, dec_ref, gn_ref,
                       og_ref, r_ref, *, n_heads, dk, dv, scale):
    @pl.when(pl.program_id(1) == 0)
    def _():
        r_ref[0] = jnp.zeros(r_ref.shape[1:], F32)

    cos, sin = cos_ref[...], sin_ref[...]
    for h in range(n_heads):
        kcols = slice(h * dk, (h + 1) * dk)
        vcols = slice(h * dv, (h + 1) * dv)
        q = _rope(q_ref[0, :, kcols], cos, sin, dk // 2)
        k = _rope(k_ref[0, :, kcols], cos, sin, dk // 2) * scale
        vb = v_ref[0, :, vcols].astype(BF16)
        r = r_ref[0, h]
        sc = lax.dot_general(q.astype(BF16), k.astype(BF16), _NT, preferred_element_type=F32) * dm_ref[h]
        o = (jnp.dot(sc.astype(BF16), vb, preferred_element_type=F32)
             + jnp.dot((q * xi_ref[h]).astype(BF16), r.astype(BF16), preferred_element_type=F32))
        r_ref[0, h] = r * dec_ref[h] + lax.dot_general((k * zt_ref[h]).astype(BF16), vb, _TN,
                                                       preferred_element_type=F32)
        og_ref[0, :, vcols] = _group_norm_gate(o, gn_ref[:, vcols], g_ref[0, :, vcols]).astype(BF16)


def _log_gamma(n_heads):
    return np.log(np.float32(1.0) - np.float32(2.0) ** (np.float32(-5.0) - np.arange(n_heads, dtype=np.float32)))


def _retention_tables(n_heads, chunk, dk, dv):
    lg = _log_gamma(n_heads)
    idx = np.arange(chunk, dtype=np.float32)
    diff = idx[:, None] - idx[None, :]
    dmask = np.where(diff[None] >= 0, np.exp(np.maximum(diff, np.float32(0.0))[None] * lg[:, None, None]),
                     np.float32(0.0)).astype(np.float32)
    xi = np.exp((idx + np.float32(1.0))[None, :] * lg[:, None])
    zeta = np.exp((np.float32(chunk - 1.0) - idx)[None, :] * lg[:, None])
    dec = np.exp(np.float32(chunk) * lg)
    return (dmask,
            np.ascontiguousarray(np.broadcast_to(xi[:, :, None], (n_heads, chunk, dk))),
            np.ascontiguousarray(np.broadcast_to(zeta[:, :, None], (n_heads, chunk, dk))),
            np.ascontiguousarray(np.broadcast_to(dec[:, None, None], (n_heads, 1, dv))))


def _rope_tables(pos, half):
    inv = np.float32(ROPE_BASE) ** (-np.arange(half, dtype=np.float32) / np.float32(half))
    ang = pos.astype(np.float32)[:, None] * inv[None, :]
    return np.cos(ang).astype(np.float32), np.sin(ang).astype(np.float32)


def _ret_prompt(proj, gn_w, n_heads, dk, dv):
    b, l, _ = proj.shape
    d = n_heads * dk
    chunk = math.gcd(l, RET_CHUNK)
    half = dk // 2
    cos, sin = _rope_tables(np.arange(l), half)
    dmask, xiq, ztk, dec = _retention_tables(n_heads, chunk, dk, dv)
    dvs = n_heads * dv
    assert dvs == 2 * d
    const = lambda shape: pl.BlockSpec(shape, lambda bi, c: (0,) * len(shape))
    og, r = pl.pallas_call(
        functools.partial(_ret_prompt_kernel, n_heads=n_heads, dk=dk, dv=dv, scale=dk ** -0.5),
        grid=(b, l // chunk),
        in_specs=[pl.BlockSpec((1, chunk, d), lambda bi, c: (bi, c, 0)),
                  pl.BlockSpec((1, chunk, d), lambda bi, c: (bi, c, 1)),
                  pl.BlockSpec((1, chunk, dvs), lambda bi, c: (bi, c, 1)),
                  pl.BlockSpec((1, chunk, dvs), lambda bi, c: (bi, c, 2)),
                  pl.BlockSpec((chunk, half), lambda bi, c: (c, 0)),
                  pl.BlockSpec((chunk, half), lambda bi, c: (c, 0)),
                  const((n_heads, chunk, chunk)), const((n_heads, chunk, dk)), const((n_heads, chunk, dk)),
                  const((n_heads, 1, dv)), const((1, dvs))],
        out_specs=[pl.BlockSpec((1, chunk, dvs), lambda bi, c: (bi, c, 0)),
                   pl.BlockSpec((1, n_heads, dk, dv), lambda bi, c: (bi, 0, 0, 0))],
        out_shape=[jax.ShapeDtypeStruct((b, l, dvs), BF16),
                   jax.ShapeDtypeStruct((b, n_heads, dk, dv), F32)],
        compiler_params=_params(2),
        name="retention_prompt",
    )(proj, proj, proj, proj, cos, sin, dmask, xiq, ztk, dec, gn_w[None, :])
    return og, r


def _ret_sample_kernel(q_ref, k_ref, v_ref, g_ref, st_ref, cos_ref, sin_ref, xi_ref, dec_ref, gn_ref,
                       og_ref, rn_ref, qpad, kpad, o_sc, *, n_heads, half, scale):
    @pl.when(pl.program_id(0) == 0)
    def _():
        qpad[...] = jnp.zeros(qpad.shape, F32)
        kpad[...] = jnp.zeros(kpad.shape, F32)

    cos, sin = cos_ref[...], sin_ref[...]
    q = _rope(q_ref[0], cos, sin, half)
    k = _rope(k_ref[0], cos, sin, half) * scale
    qpad[0:n_heads, :] = q
    kpad[0:n_heads, :] = k
    q_t = qpad[...].T
    k_t = kpad[...].T
    qk = jnp.sum(q * k, axis=-1, keepdims=True)
    v = v_ref[0]
    for h in range(n_heads):
        rh = st_ref[0, h]
        vh = v[h:h + 1, :]
        qr = jnp.sum(rh * q_t[:, h:h + 1], axis=0, keepdims=True)
        o_sc[h:h + 1, :] = qk[h:h + 1, :] * vh + xi_ref[h:h + 1, :] * qr
        rn_ref[0, h] = rh * dec_ref[h:h + 1, :] + k_t[:, h:h + 1] * vh
    og_ref[0] = _group_norm_gate(o_sc[...], gn_ref[...], g_ref[0]).astype(BF16)


def _ret_sample(proj, state, j, prev_new_state, gn_w, n_heads, dk, dv):
    _, bs, n6 = proj.shape
    half = dk // 2
    cos, sin = _rope_tables(np.full((1,), PAST_LEN), half)
    gamma = np.ascontiguousarray(np.broadcast_to(np.exp(_log_gamma(n_heads))[:, None], (n_heads, dv)))
    qk_view = proj.reshape(bs, n6 // dk, dk)
    vg_view = proj.reshape(bs, n6 // dv, dv)
    in_specs = [pl.BlockSpec((1, n_heads, dk), lambda b: (b, 0, 0)),
                pl.BlockSpec((1, n_heads, dk), lambda b: (b, 1, 0)),
                pl.BlockSpec((1, n_heads, dv), lambda b: (b, 1, 0)),
                pl.BlockSpec((1, n_heads, dv), lambda b: (b, 2, 0)),
                pl.BlockSpec((None, 1, n_heads, dk, dv), lambda b: (j, b, 0, 0, 0)),
                pl.BlockSpec((1, half), lambda b: (0, 0)),
                pl.BlockSpec((1, half), lambda b: (0, 0)),
                pl.BlockSpec((n_heads, dv), lambda b: (0, 0)),
                pl.BlockSpec((n_heads, dv), lambda b: (0, 0)),
                pl.BlockSpec((n_heads, dv), lambda b: (0, 0))]
    args = [qk_view, qk_view, vg_view, vg_view, state, cos, sin, gamma, gamma, gn_w.reshape(n_heads, dv)]
    aliases = {}
    kern = functools.partial(_ret_sample_kernel, n_heads=n_heads, half=half, scale=dk ** -0.5)
    if prev_new_state is not None:
        in_specs.append(pl.BlockSpec(memory_space=pl.ANY))
        args.append(prev_new_state)
        aliases = {len(args) - 1: 1}
        inner = kern
        kern = lambda *refs: inner(*refs[:10], *refs[11:])
    og, new_state = pl.pallas_call(
        kern,
        grid=(bs,),
        in_specs=in_specs,
        out_specs=[pl.BlockSpec((1, n_heads, dv), lambda b: (b, 0, 0)),
                   pl.BlockSpec((None, 1, n_heads, dk, dv), lambda b: (j, b, 0, 0, 0))],
        out_shape=[jax.ShapeDtypeStruct((bs, n_heads, dv), BF16),
                   jax.ShapeDtypeStruct(state.shape, state.dtype)],
        scratch_shapes=[pltpu.VMEM((LANES, dk), F32), pltpu.VMEM((LANES, dk), F32),
                        pltpu.VMEM((n_heads, dv), F32)],
        input_output_aliases=aliases,
        compiler_params=_params(1),
        name="retention_sample",
    )(*args)
    return og.reshape(1, bs, n_heads * dv), new_state


def _layer_norm_silu(y, w, b):
    mu = jnp.mean(y, axis=-1, keepdims=True)
    d = y - mu
    var = jnp.mean(d * d, axis=-1, keepdims=True)
    return _silu(d * lax.rsqrt(var + NORM_EPS) * w + b)


def _conv_prompt_kernel(u_ref, wdw_ref, bdw_ref, lnw_ref, lnb_ref, z_ref, nc_ref, ext, zs, ys, *, width, tl, pad):
    li = pl.program_id(1)

    @pl.when(li == 0)
    def _():
        ext[0:pad, :] = jnp.zeros((pad, ext.shape[1]), F32)

    @pl.when(li > 0)
    def _():
        ext[0:pad, :] = ext[tl:tl + pad, :]

    ext[pad:pad + tl, :] = u_ref[0]
    off = pad - (width - 1)
    for s0 in range(0, ext.shape[1], LANES):
        cols = slice(s0, s0 + LANES)
        y = None
        for b in range(8):
            rows = tl if b == 0 else tl + 8
            acc = None
            for c in range(off, off + width):
                if c % 8 == b:
                    term = wdw_ref[c - off:c - off + 1, cols] * ext[c - b:c - b + rows, cols]
                    acc = term if acc is None else acc + term
            if b == 0:
                y = bdw_ref[:, cols] + acc
            else:
                zs[0:rows, cols] = acc
                y = y + zs[b:b + tl, cols]
        ys[:, cols] = y
    z_ref[0] = _layer_norm_silu(ys[...], lnw_ref[...], lnb_ref[...]).astype(BF16)

    @pl.when(li == pl.num_programs(1) - 1)
    def _():
        nc_ref[0] = ext[tl + off:tl + pad, :]


def _conv_prompt(u, w_dw, b_dw, ln_w, ln_b, j):
    b, l, d = u.shape
    width = w_dw.shape[1]
    tl = min(l, 128)
    pad = 32
    vec = lambda a: a[:, None, :]
    return pl.pallas_call(
        functools.partial(_conv_prompt_kernel, width=width, tl=tl, pad=pad),
        grid=(b, l // tl),
        in_specs=[pl.BlockSpec((1, tl, d), lambda bi, li: (bi, li, 0)),
                  pl.BlockSpec((None, width, d), lambda bi, li: (j, 0, 0)),
                  pl.BlockSpec((None, 1, d), lambda bi, li: (j, 0, 0)),
                  pl.BlockSpec((None, 1, d), lambda bi, li: (j, 0, 0)),
                  pl.BlockSpec((None, 1, d), lambda bi, li: (j, 0, 0))],
        out_specs=[pl.BlockSpec((1, tl, d), lambda bi, li: (bi, li, 0)),
                   pl.BlockSpec((1, width - 1, d), lambda bi, li: (bi, 0, 0))],
        out_shape=[jax.ShapeDtypeStruct((b, l, d), BF16),
                   jax.ShapeDtypeStruct((b, width - 1, d), F32)],
        scratch_shapes=[pltpu.VMEM((tl + pad, d), F32), pltpu.VMEM((tl + 8, d), F32), pltpu.VMEM((tl, d), F32)],
        compiler_params=_params(2),
        name="conv_prompt",
    )(u, w_dw, vec(b_dw), vec(ln_w), vec(ln_b))


def _conv_sample_kernel(st_ref, u_ref, wdw_ref, bdw_ref, lnw_ref, lnb_ref, z_ref, nc_ref, *, width):
    st = st_ref[...]
    u = u_ref[...]
    y = (jnp.sum(st * wdw_ref[0:width - 1, :], axis=0, keepdims=True)
         + u * wdw_ref[width - 1:width, :] + bdw_ref[...])
    z_ref[...] = _layer_norm_silu(y, lnw_ref[...], lnb_ref[...]).astype(BF16)
    nc_ref[0:width - 2, :] = st[1:width - 1, :]
    nc_ref[width - 2:width - 1, :] = u


def _conv_sample(u, state, w_dw, b_dw, ln_w, ln_b, j):
    _, bs, d = u.shape
    width = w_dw.shape[1]
    vec = lambda a: a[:, None, :]
    z, nc = pl.pallas_call(
        functools.partial(_conv_sample_kernel, width=width),
        grid=(bs,),
        in_specs=[pl.BlockSpec((None, None, width - 1, d), lambda b: (j, b, 0, 0)),
                  pl.BlockSpec((None, 1, d), lambda b: (b, 0, 0)),
                  pl.BlockSpec((None, width, d), lambda b: (j, 0, 0)),
                  pl.BlockSpec((None, 1, d), lambda b: (j, 0, 0)),
                  pl.BlockSpec((None, 1, d), lambda b: (j, 0, 0)),
                  pl.BlockSpec((None, 1, d), lambda b: (j, 0, 0))],
        out_specs=[pl.BlockSpec((None, 1, d), lambda b: (b, 0, 0)),
                   pl.BlockSpec((None, width - 1, d), lambda b: (b, 0, 0))],
        out_shape=[jax.ShapeDtypeStruct((bs, 1, d), BF16),
                   jax.ShapeDtypeStruct((bs, width - 1, d), state.dtype)],
        compiler_params=_params(1),
        name="conv_sample",
    )(state, u.reshape(bs, 1, d), w_dw, vec(b_dw), vec(ln_w), vec(ln_b))
    return z.reshape(1, bs, d), nc


def _top_rows(s, count):
    rows = []
    for _ in range(count):
        m = jnp.max(s, axis=0, keepdims=True)
        rows.append(m)
        s = jnp.where(s == m, -jnp.inf, s)
    return rows


def _stack_rows(rows, base):
    sub = lax.broadcasted_iota(jnp.int32, (8, rows[0].shape[1]), 0)
    out = jnp.broadcast_to(rows[base + 7], sub.shape)
    for r in range(6, -1, -1):
        out = jnp.where(sub == r, rows[base + r], out)
    return out


def _candidates(a, b, b_lo, b_hi, topk):
    assert topk == 16
    sub = lax.broadcasted_iota(jnp.int32, (8, a[0].shape[1]), 0)
    a_hi = _stack_rows(a, 8)
    parts = [a[0] + b_lo, a[0] + b_hi]
    for r1 in range(1, 8):
        n = topk // (r1 + 1)
        c = a[r1] + b_lo
        parts.append(c if n >= 8 else jnp.where(sub < n, c, -jnp.inf))
    parts.append(a_hi + b[0])
    return parts


def _kth_largest(parts, k):
    parts = list(parts)
    for _ in range(k):
        m = parts[0]
        for p in parts[1:]:
            m = jnp.maximum(m, p)
        m = jnp.max(m, axis=0, keepdims=True)
        parts = [jnp.where(p == m, -jnp.inf, p) for p in parts]
    return m


def _route_kernel(h_ref, wq_ref, keys_ref, s1_ref, s2_ref, e1_ref, e2_ref, thr_ref, bl_ref, s_sc, *, half, topk, tr):
    q = jnp.dot(h_ref[0], wq_ref[...].astype(BF16), preferred_element_type=F32).astype(BF16)
    s_sc[0] = lax.dot_general(keys_ref[0, 0].astype(BF16), q[:, :half], _NT, preferred_element_type=F32)
    s_sc[1] = lax.dot_general(keys_ref[0, 1].astype(BF16), q[:, half:], _NT, preferred_element_type=F32)
    for lt in range(tr // LANES):
        lanes = slice(lt * LANES, (lt + 1) * LANES)
        s1 = s_sc[0, :, lanes]
        s2 = s_sc[1, :, lanes]
        a = _top_rows(s1, topk)
        b = _top_rows(s2, topk)
        b_lo, b_hi = _stack_rows(b, 0), _stack_rows(b, 8)
        parts = _candidates(a, b, b_lo, b_hi, topk)
        thr = _kth_largest(parts, topk)
        cmax = a[0] + b[0]
        z = jnp.zeros_like(thr)
        for c in parts:
            z = z + jnp.sum(jnp.where(c >= thr, jnp.exp(c - cmax), 0.0), axis=0, keepdims=True)
        s1_ref[0, 0, lt] = s1
        s2_ref[0, 0, lt] = s2
        e1_ref[0, 0, lt] = jnp.exp(s1 - a[0]) * (1.0 / z)
        e2_ref[0, 0, lt] = jnp.exp(s2 - b[0]).astype(BF16)
        thr_ref[0, 0, lt] = thr
        bl_ref[0, 0, lt, 0:8, :] = b_lo
        bl_ref[0, 0, lt, 8:16, :] = b_hi


def _route(h, w_q, keys, layer):
    g, r, d = h.shape
    _, hp, _, nk, half = keys.shape
    dq = 2 * half
    tr = min(r, ROUTE_TILE)
    nlt = tr // LANES
    tab = jax.ShapeDtypeStruct((g, hp, r // LANES, nk, LANES), F32)
    tab_spec = pl.BlockSpec((1, 1, nlt, nk, LANES), lambda gi, ti, hi: (gi, hi, ti, 0, 0))
    return pl.pallas_call(
        functools.partial(_route_kernel, half=half, topk=PEER_TOPK, tr=tr),
        grid=(g, r // tr, hp),
        in_specs=[pl.BlockSpec((1, tr, d), lambda gi, ti, hi: (gi, ti, 0)),
                  pl.BlockSpec((None, d, dq), lambda gi, ti, hi: (layer, 0, hi)),
                  pl.BlockSpec((None, 1, 2, nk, half), lambda gi, ti, hi: (layer, hi, 0, 0, 0))],
        out_specs=[tab_spec, tab_spec, tab_spec, tab_spec,
                   pl.BlockSpec((1, 1, nlt, 1, LANES), lambda gi, ti, hi: (gi, hi, ti, 0, 0)),
                   pl.BlockSpec((1, 1, nlt, PEER_TOPK, LANES), lambda gi, ti, hi: (gi, hi, ti, 0, 0))],
        out_shape=[tab, tab, tab, jax.ShapeDtypeStruct(tab.shape, BF16),
                   jax.ShapeDtypeStruct((g, hp, r // LANES, 1, LANES), F32),
                   jax.ShapeDtypeStruct((g, hp, r // LANES, PEER_TOPK, LANES), F32)],
        scratch_shapes=[pltpu.VMEM((2, nk, tr), F32)],
        compiler_params=_params(3),
        name="peer_route",
    )(h, w_q, keys)


_ERF_NUM = (-2.72614225801306e-10, 2.77068142495902e-08, -2.10102402082508e-06, -5.69250639462346e-05,
            -7.34990630326855e-04, -2.95459980854025e-03, -1.60960333262415e-02)
_ERF_DEN = (-1.45660718464996e-05, -2.13374055278905e-04, -1.68282697438203e-03, -7.37332916720468e-03,
            -1.42647390514189e-02)


_GELU_NUM = tuple(c * 0.5 ** (len(_ERF_NUM) - 1 - m) * 0.5 * 2.0 ** -0.5 for m, c in enumerate(_ERF_NUM))
_GELU_DEN = tuple(c * 0.5 ** (len(_ERF_DEN) - 1 - m) for m, c in enumerate(_ERF_DEN))
_GELU_CLAMP = 4.0 * 2.0 ** 0.5


def _gelu(a):
    ac = jnp.clip(a, -_GELU_CLAMP, _GELU_CLAMP)
    a2 = ac * ac
    p = _GELU_NUM[0] * a2 + _GELU_NUM[1]
    for c in _GELU_NUM[2:]:
        p = p * a2 + c
    q = _GELU_DEN[0] * a2 + _GELU_DEN[1]
    for c in _GELU_DEN[2:]:
        q = q * a2 + c
    return a * (0.5 + ac * p / q)


def _peer_dense_kernel(h_ref, u_ref, v_ref, s1_ref, s2_ref, e1_ref, e2_ref, thr_ref, bl_ref, x_ref, gate_ref,
                       o_ref, a_sc, w_sc, *, n_heads, nk, tt, eb):
    e = pl.program_id(2)

    @pl.when(e == 0)
    def _():
        o_ref[0] = jnp.zeros(o_ref.shape[1:], F32)

    a_sc[...] = lax.dot_general(u_ref[...], h_ref[0], _NT, preferred_element_type=F32)

    rows_per = 16
    tile = (rows_per, LANES)
    for lt in range(tt // LANES):
        lanes = slice(lt * LANES, (lt + 1) * LANES)
        t_rows = []
        for h in range(n_heads):
            s1_blk = s1_ref[0, h, lt]
            thr_row = thr_ref[0, h, lt]
            t = jnp.full(s1_blk.shape, jnp.inf, F32)
            for r in range(bl_ref.shape[3]):
                bv = bl_ref[0, h, lt, r:r + 1, :]
                t = jnp.where(s1_blk + bv >= thr_row, bv, t)
            t_rows.append(t)
        for ii in range(eb // nk):
            t_b = [jnp.broadcast_to(t_rows[h][ii:ii + 1, :], tile) for h in range(n_heads)]
            e1_b = [jnp.broadcast_to(e1_ref[0, h, lt, ii:ii + 1, :], tile).astype(BF16) for h in range(n_heads)]
            for jc in range(nk // rows_per):
                jrows = slice(jc * rows_per, (jc + 1) * rows_per)
                rows = slice(ii * nk + jc * rows_per, ii * nk + (jc + 1) * rows_per)
                gsum = jnp.zeros(tile, BF16)
                for h in range(n_heads):
                    hit = s2_ref[0, h, lt, jrows, :] >= t_b[h]
                    gsum = gsum + jnp.where(hit, e1_b[h] * e2_ref[0, h, lt, jrows, :], jnp.zeros(tile, BF16))
                w_sc[rows, lanes] = (gsum.astype(F32) * _gelu(a_sc[rows, lanes])).astype(BF16)

    o_ref[0] += lax.dot_general(w_sc[...], v_ref[...], _TN, preferred_element_type=F32)

    @pl.when(e == pl.num_programs(2) - 1)
    def _():
        o_ref[0] = x_ref[0] + gate_ref[0] * o_ref[0]


def _peer_dense(h, u, v, layer, tables, x, mod, gate_chunk):
    g, r, d = h.shape
    s1, s2, e1, e2, thr, bl = tables
    hp, nk = s1.shape[1], s1.shape[3]
    tt = min(r, ROW_TILE)
    eb = EXPERT_TILE
    per = eb // nk
    assert tt % LANES == 0 and per == 8
    nlt = tt // LANES
    full_spec = pl.BlockSpec((1, hp, nlt, nk, LANES), lambda gi, ti, ei: (gi, 0, ti, 0, 0))
    rows_spec = pl.BlockSpec((1, hp, nlt, per, LANES), lambda gi, ti, ei: (gi, 0, ti, ei, 0))
    return pl.pallas_call(
        functools.partial(_peer_dense_kernel, n_heads=hp, nk=nk, tt=tt, eb=eb),
        grid=(g, r // tt, u.shape[1] // eb),
        in_specs=[pl.BlockSpec((1, tt, d), lambda gi, ti, ei: (gi, ti, 0)),
                  pl.BlockSpec((None, eb, d), lambda gi, ti, ei: (layer, ei, 0)),
                  pl.BlockSpec((None, eb, d), lambda gi, ti, ei: (layer, ei, 0)),
                  rows_spec, full_spec, rows_spec, full_spec,
                  pl.BlockSpec((1, hp, nlt, 1, LANES), lambda gi, ti, ei: (gi, 0, ti, 0, 0)),
                  pl.BlockSpec((1, hp, nlt, bl.shape[3], LANES), lambda gi, ti, ei: (gi, 0, ti, 0, 0)),
                  pl.BlockSpec((1, tt, d), lambda gi, ti, ei: (gi, ti, 0)),
                  _mod_spec(mod, tt, gate_chunk, lambda gi, ti, ei: (gi, ti))],
        out_specs=pl.BlockSpec((1, tt, d), lambda gi, ti, ei: (gi, ti, 0)),
        out_shape=jax.ShapeDtypeStruct((g, r, d), F32),
        scratch_shapes=[pltpu.VMEM((eb, tt), F32), pltpu.VMEM((eb, tt), BF16)],
        compiler_params=_params(3),
        name="peer_dense",
    )(h, u, v, s1, s2, e1, e2, thr, bl, x, mod)


def _trunk(x, mods, is_prompt, state_ret, state_conv, norm_w, final_norm_w,
           ret_w_in, ret_gn_w, ret_w_out, conv_w_pw1, conv_b_pw1, conv_w_dw, conv_b_dw,
           conv_ln_w, conv_ln_b, conv_w_pw2, conv_b_pw2, peer_w_q, peer_keys, peer_u, peer_v):
    depth = norm_w.shape[0]
    n_heads, dk, dv = state_ret.shape[2:]
    new_ret, new_conv = None, []
    ret_prompt_states = []
    vec = lambda a: a[:, None, :]
    for i in range(depth):
        mod = mods[i]
        h = _modnorm(x, norm_w[i, 0], mod, 0, 1)
        j = i // 2
        if i % 2 == 0:
            proj = _mm_plain(h, ret_w_in, j)
            if is_prompt:
                og, r = _ret_prompt(proj, ret_gn_w[j], n_heads, dk, dv)
                ret_prompt_states.append(r)
            else:
                og, new_ret = _ret_sample(proj, state_ret, j, new_ret, ret_gn_w[j], n_heads, dk, dv)
            x = _mm_resid(og, ret_w_out, None, j, x, mod, 2)
        else:
            u = _mm_glu(h, conv_w_pw1, vec(conv_b_pw1), j)
            if is_prompt:
                z, nc = _conv_prompt(u, conv_w_dw, conv_b_dw, conv_ln_w, conv_ln_b, j)
            else:
                z, nc = _conv_sample(u, state_conv, conv_w_dw, conv_b_dw, conv_ln_w, conv_ln_b, j)
            new_conv.append(nc)
            x = _mm_resid(z, conv_w_pw2, vec(conv_b_pw2), j, x, mod, 2)
        h = _modnorm(x, norm_w[i, 1], mod, 3, 4)
        tables = _route(h, peer_w_q, peer_keys, i)
        x = _peer_dense(h, peer_u, peer_v, i, tables, x, mod, 5)
    y = _rmsnorm(x, final_norm_w)
    if is_prompt:
        new_ret = jnp.stack(ret_prompt_states)
    return y, new_ret, jnp.stack(new_conv)


def kernel(x_prompt, x_sample, state_ret, state_conv, c_prompt, c_sample, ada_w, ada_b, norm_w, final_norm_w,
           ret_w_in, ret_gn_w, ret_w_out, conv_w_pw1, conv_b_pw1, conv_w_dw, conv_b_dw, conv_ln_w, conv_ln_b,
           conv_w_pw2, conv_b_pw2, peer_w_q, peer_keys, peer_u, peer_v):
    b, l, d = x_prompt.shape
    bs, ls, _ = x_sample.shape
    assert ls == 1, "the sample path handles one new token per sequence"
    depth = ada_w.shape[0]

    n_c = b + bs
    n_pad = -n_c % 8
    c_all = jnp.concatenate([c_prompt, c_sample, jnp.zeros((n_pad, d), c_prompt.dtype)], axis=0)
    mod = _ada(c_all, ada_w, ada_b)
    mods_p = [mod[i, :b].reshape(b, 1, 6 * d) for i in range(depth)]
    mods_s = [mod[i, b:b + bs].reshape(1, bs, 6 * d) for i in range(depth)]

    weights = (norm_w, final_norm_w, ret_w_in, ret_gn_w, ret_w_out, conv_w_pw1, conv_b_pw1, conv_w_dw, conv_b_dw,
               conv_ln_w, conv_ln_b, conv_w_pw2, conv_b_pw2, peer_w_q, peer_keys,
               peer_u.astype(BF16), peer_v.astype(BF16))
    y_p, ret_p, conv_p = _trunk(x_prompt, mods_p, True, state_ret, state_conv, *weights)
    y_s, ret_s, conv_s = _trunk(x_sample.reshape(1, bs, d), mods_s, False, state_ret, state_conv, *weights)
    return (y_p, y_s.reshape(bs, 1, d), ret_p, conv_p, ret_s, conv_s)
```

```python
import functools
import math

import jax
import jax.numpy as jnp
import numpy as np
from jax import lax
from jax.experimental import pallas as pl
from jax.experimental.pallas import tpu as pltpu

F32 = jnp.float32
BF16 = jnp.bfloat16

NORM_EPS = 1e-6
ROPE_BASE = 10000.0
RET_CHUNK = 128
PEER_TOPK = 16
PAST_LEN = 16384

LANES = 128
VMEM_LIMIT_BYTES = 56 * 1024 * 1024
ROW_TILE = 512
COL_TILE = 512
EXPERT_TILE = 1024
ROUTE_TILE = 512

_NT = (((1,), (1,)), ((), ()))
_TN = (((0,), (0,)), ((), ()))


def _params(n_axes):
    return pltpu.CompilerParams(dimension_semantics=("arbitrary",) * n_axes,
                                vmem_limit_bytes=VMEM_LIMIT_BYTES)


def _silu(x):
    return x * (1.0 / (1.0 + jnp.exp(-x)))


def _mod_spec(mod, tile, chunk, index_of):
    rm = mod.shape[1]
    d = mod.shape[2] // 6
    if rm == 1:
        return pl.BlockSpec((1, 1, d), lambda *ids: (index_of(*ids)[0], 0, chunk))
    return pl.BlockSpec((1, tile, d), lambda *ids: (index_of(*ids)[0], index_of(*ids)[1], chunk))


def _ada_kernel(c_ref, w_ref, b_ref, o_ref):
    sc = _silu(c_ref[...]).astype(BF16)
    o_ref[...] = jnp.dot(sc, w_ref[...].astype(BF16), preferred_element_type=F32) + b_ref[...]


def _ada(c_all, ada_w, ada_b):
    m, d = c_all.shape
    nl, _, n = ada_w.shape
    tn = min(n, 1024)
    return pl.pallas_call(
        _ada_kernel,
        grid=(nl, n // tn),
        in_specs=[pl.BlockSpec((m, d), lambda l, j: (0, 0)),
                  pl.BlockSpec((None, d, tn), lambda l, j: (l, 0, j)),
                  pl.BlockSpec((None, 1, tn), lambda l, j: (l, 0, j))],
        out_specs=pl.BlockSpec((None, m, tn), lambda l, j: (l, 0, j)),
        out_shape=jax.ShapeDtypeStruct((nl, m, n), F32),
        compiler_params=_params(2),
        name="ada_mod",
    )(c_all, ada_w, ada_b[:, None, :])


def _modnorm_kernel(x_ref, w_ref, sh_ref, sc_ref, o_ref):
    x = x_ref[0]
    y = x * lax.rsqrt(jnp.mean(x * x, axis=-1, keepdims=True) + NORM_EPS) * w_ref[...]
    o_ref[0] = (y * (1.0 + sc_ref[0]) + sh_ref[0]).astype(o_ref.dtype)


def _modnorm(x, w, mod, shift_chunk, scale_chunk):
    g, r, d = x.shape
    tr = min(r, ROW_TILE)
    ix = lambda gi, ti: (gi, ti)
    return pl.pallas_call(
        _modnorm_kernel,
        grid=(g, r // tr),
        in_specs=[pl.BlockSpec((1, tr, d), lambda gi, ti: (gi, ti, 0)),
                  pl.BlockSpec((1, d), lambda gi, ti: (0, 0)),
                  _mod_spec(mod, tr, shift_chunk, ix),
                  _mod_spec(mod, tr, scale_chunk, ix)],
        out_specs=pl.BlockSpec((1, tr, d), lambda gi, ti: (gi, ti, 0)),
        out_shape=jax.ShapeDtypeStruct((g, r, d), BF16),
        compiler_params=_params(2),
        name="modnorm",
    )(x, w[None, :], mod, mod)


def _rmsnorm_kernel(x_ref, w_ref, o_ref):
    x = x_ref[0]
    o_ref[0] = x * lax.rsqrt(jnp.mean(x * x, axis=-1, keepdims=True) + NORM_EPS) * w_ref[...]


def _rmsnorm(x, w):
    g, r, d = x.shape
    tr = min(r, ROW_TILE)
    return pl.pallas_call(
        _rmsnorm_kernel,
        grid=(g, r // tr),
        in_specs=[pl.BlockSpec((1, tr, d), lambda gi, ti: (gi, ti, 0)),
                  pl.BlockSpec((1, d), lambda gi, ti: (0, 0))],
        out_specs=pl.BlockSpec((1, tr, d), lambda gi, ti: (gi, ti, 0)),
        out_shape=jax.ShapeDtypeStruct((g, r, d), F32),
        compiler_params=_params(2),
        name="final_rmsnorm",
    )(x, w[None, :])


def _first_inner_step():
    return jnp.logical_and(pl.program_id(1) == 0, pl.program_id(2) == 0)


def _mm_plain_kernel(x_ref, w_ref, o_ref, wbf):
    @pl.when(_first_inner_step())
    def _():
        wbf[...] = w_ref[...].astype(BF16)
    o_ref[0] = jnp.dot(x_ref[0], wbf[...], preferred_element_type=F32)


def _mm_plain(x, w3, layer):
    g, r, k = x.shape
    n = w3.shape[2]
    tm, tn = min(r, ROW_TILE), min(n, 2 * COL_TILE)
    return pl.pallas_call(
        _mm_plain_kernel,
        grid=(n // tn, g, r // tm),
        in_specs=[pl.BlockSpec((1, tm, k), lambda j, gi, ti: (gi, ti, 0)),
                  pl.BlockSpec((None, k, tn), lambda j, gi, ti: (layer, 0, j))],
        out_specs=pl.BlockSpec((1, tm, tn), lambda j, gi, ti: (gi, ti, j)),
        out_shape=jax.ShapeDtypeStruct((g, r, n), F32),
        scratch_shapes=[pltpu.VMEM((k, tn), BF16)],
        compiler_params=_params(3),
        name="mm_plain",
    )(x, w3)


def _mm_glu_kernel(x_ref, wa_ref, wb_ref, ba_ref, bb_ref, o_ref, wabf, wbbf):
    @pl.when(_first_inner_step())
    def _():
        wabf[...] = wa_ref[...].astype(BF16)
        wbbf[...] = wb_ref[...].astype(BF16)
    x = x_ref[0]
    a = jnp.dot(x, wabf[...], preferred_element_type=F32) + ba_ref[...]
    b = jnp.dot(x, wbbf[...], preferred_element_type=F32) + bb_ref[...]
    o_ref[0] = a * (1.0 / (1.0 + jnp.exp(-b)))


def _mm_glu(x, w3, b3, layer):
    g, r, k = x.shape
    n = w3.shape[2] // 2
    tm, tn = min(r, ROW_TILE), min(n, COL_TILE)
    nb = n // tn
    return pl.pallas_call(
        _mm_glu_kernel,
        grid=(nb, g, r // tm),
        in_specs=[pl.BlockSpec((1, tm, k), lambda j, gi, ti: (gi, ti, 0)),
                  pl.BlockSpec((None, k, tn), lambda j, gi, ti: (layer, 0, j)),
                  pl.BlockSpec((None, k, tn), lambda j, gi, ti: (layer, 0, nb + j)),
                  pl.BlockSpec((None, 1, tn), lambda j, gi, ti: (layer, 0, j)),
                  pl.BlockSpec((None, 1, tn), lambda j, gi, ti: (layer, 0, nb + j))],
        out_specs=pl.BlockSpec((1, tm, tn), lambda j, gi, ti: (gi, ti, j)),
        out_shape=jax.ShapeDtypeStruct((g, r, n), F32),
        scratch_shapes=[pltpu.VMEM((k, tn), BF16), pltpu.VMEM((k, tn), BF16)],
        compiler_params=_params(3),
        name="mm_glu",
    )(x, w3, w3, b3, b3)


def _mm_resid_kernel(x_ref, w_ref, *rest, has_bias):
    if has_bias:
        b_ref, res_ref, gate_ref, o_ref, wbf = rest
    else:
        res_ref, gate_ref, o_ref, wbf = rest

    @pl.when(_first_inner_step())
    def _():
        wbf[...] = w_ref[...].astype(BF16)
    y = jnp.dot(x_ref[0], wbf[...], preferred_element_type=F32)
    if has_bias:
        y = y + b_ref[...]
    o_ref[0] = res_ref[0] + gate_ref[0] * y


def _mm_resid(x, w3, b3, layer, res, mod, gate_chunk):
    g, r, k = x.shape
    n = w3.shape[2]
    tm, tn = min(r, ROW_TILE), min(n, COL_TILE)
    per_d = n // tn
    rm = mod.shape[1]
    if rm == 1:
        gate_spec = pl.BlockSpec((1, 1, tn), lambda j, gi, ti: (gi, 0, gate_chunk * per_d + j))
    else:
        gate_spec = pl.BlockSpec((1, tm, tn), lambda j, gi, ti: (gi, ti, gate_chunk * per_d + j))
    in_specs = [pl.BlockSpec((1, tm, k), lambda j, gi, ti: (gi, ti, 0)),
                pl.BlockSpec((None, k, tn), lambda j, gi, ti: (layer, 0, j))]
    args = [x, w3]
    if b3 is not None:
        in_specs.append(pl.BlockSpec((None, 1, tn), lambda j, gi, ti: (layer, 0, j)))
        args.append(b3)
    in_specs += [pl.BlockSpec((1, tm, tn), lambda j, gi, ti: (gi, ti, j)), gate_spec]
    args += [res, mod]
    return pl.pallas_call(
        functools.partial(_mm_resid_kernel, has_bias=b3 is not None),
        grid=(n // tn, g, r // tm),
        in_specs=in_specs,
        out_specs=pl.BlockSpec((1, tm, tn), lambda j, gi, ti: (gi, ti, j)),
        out_shape=jax.ShapeDtypeStruct((g, r, n), F32),
        scratch_shapes=[pltpu.VMEM((k, tn), BF16)],
        compiler_params=_params(3),
        name="mm_resid",
    )(*args)


def _rope(x, cos, sin, half):
    x1, x2 = x[:, :half], x[:, half:]
    return jnp.concatenate([x1 * cos - x2 * sin, x1 * sin + x2 * cos], axis=-1)


def _group_norm_gate(o, gn_w, g):
    mu = jnp.mean(o, axis=-1, keepdims=True)
    d = o - mu
    var = jnp.mean(d * d, axis=-1, keepdims=True)
    return d * lax.rsqrt(var + NORM_EPS) * gn_w * _silu(g)


def _ret_prompt_kernel(q_ref, k_ref, v_ref, g_ref, cos_ref, sin_ref, dm_ref, xi_ref, zt_ref, dec_ref, gn_ref,
                       og_ref, r_ref, *, n_heads, dk, dv, scale):
    @pl.when(pl.program_id(1) == 0)
    def _():
        r_ref[0] = jnp.zeros(r_ref.shape[1:], F32)

    cos, sin = cos_ref[...], sin_ref[...]
    for h in range(n_heads):
        kcols = slice(h * dk, (h + 1) * dk)
        vcols = slice(h * dv, (h + 1) * dv)
        q = _rope(q_ref[0, :, kcols], cos, sin, dk // 2)
        k = _rope(k_ref[0, :, kcols], cos, sin, dk // 2) * scale
        vb = v_ref[0, :, vcols].astype(BF16)
        r = r_ref[0, h]
        sc = lax.dot_general(q.astype(BF16), k.astype(BF16), _NT, preferred_element_type=F32) * dm_ref[h]
        o = (jnp.dot(sc.astype(BF16), vb, preferred_element_type=F32)
             + jnp.dot((q * xi_ref[h]).astype(BF16), r.astype(BF16), preferred_element_type=F32))
        r_ref[0, h] = r * dec_ref[h] + lax.dot_general((k * zt_ref[h]).astype(BF16), vb, _TN,
                                                       preferred_element_type=F32)
        og_ref[0, :, vcols] = _group_norm_gate(o, gn_ref[:, vcols], g_ref[0, :, vcols]).astype(BF16)


def _log_gamma(n_heads):
    return np.log(np.float32(1.0) - np.float32(2.0) ** (np.float32(-5.0) - np.arange(n_heads, dtype=np.float32)))


def _retention_tables(n_heads, chunk, dk, dv):
    lg = _log_gamma(n_heads)
    idx = np.arange(chunk, dtype=np.float32)
    diff = idx[:, None] - idx[None, :]
    dmask = np.where(diff[None] >= 0, np.exp(np.maximum(diff, np.float32(0.0))[None] * lg[:, None, None]),
                     np.float32(0.0)).astype(np.float32)
    xi = np.exp((idx + np.float32(1.0))[None, :] * lg[:, None])
    zeta = np.exp((np.float32(chunk - 1.0) - idx)[None, :] * lg[:, None])
    dec = np.exp(np.float32(chunk) * lg)
    return (dmask,
            np.ascontiguousarray(np.broadcast_to(xi[:, :, None], (n_heads, chunk, dk))),
            np.ascontiguousarray(np.broadcast_to(zeta[:, :, None], (n_heads, chunk, dk))),
            np.ascontiguousarray(np.broadcast_to(dec[:, None, None], (n_heads, 1, dv))))


def _rope_tables(pos, half):
    inv = np.float32(ROPE_BASE) ** (-np.arange(half, dtype=np.float32) / np.float32(half))
    ang = pos.astype(np.float32)[:, None] * inv[None, :]
    return np.cos(ang).astype(np.float32), np.sin(ang).astype(np.float32)


def _ret_prompt(proj, gn_w, n_heads, dk, dv):
    b, l, _ = proj.shape
    d = n_heads * dk
    chunk = math.gcd(l, RET_CHUNK)
    half = dk // 2
    cos, sin = _rope_tables(np.arange(l), half)
    dmask, xiq, ztk, dec = _retention_tables(n_heads, chunk, dk, dv)
    dvs = n_heads * dv
    assert dvs == 2 * d
    const = lambda shape: pl.BlockSpec(shape, lambda bi, c: (0,) * len(shape))
    og, r = pl.pallas_call(
        functools.partial(_ret_prompt_kernel, n_heads=n_heads, dk=dk, dv=dv, scale=dk ** -0.5),
        grid=(b, l // chunk),
        in_specs=[pl.BlockSpec((1, chunk, d), lambda bi, c: (bi, c, 0)),
                  pl.BlockSpec((1, chunk, d), lambda bi, c: (bi, c, 1)),
                  pl.BlockSpec((1, chunk, dvs), lambda bi, c: (bi, c, 1)),
                  pl.BlockSpec((1, chunk, dvs), lambda bi, c: (bi, c, 2)),
                  pl.BlockSpec((chunk, half), lambda bi, c: (c, 0)),
                  pl.BlockSpec((chunk, half), lambda bi, c: (c, 0)),
                  const((n_heads, chunk, chunk)), const((n_heads, chunk, dk)), const((n_heads, chunk, dk)),
                  const((n_heads, 1, dv)), const((1, dvs))],
        out_specs=[pl.BlockSpec((1, chunk, dvs), lambda bi, c: (bi, c, 0)),
                   pl.BlockSpec((1, n_heads, dk, dv), lambda bi, c: (bi, 0, 0, 0))],
        out_shape=[jax.ShapeDtypeStruct((b, l, dvs), BF16),
                   jax.ShapeDtypeStruct((b, n_heads, dk, dv), F32)],
        compiler_params=_params(2),
        name="retention_prompt",
    )(proj, proj, proj, proj, cos, sin, dmask, xiq, ztk, dec, gn_w[None, :])
    return og, r


def _ret_sample_kernel(q_ref, k_ref, v_ref, g_ref, st_ref, cos_ref, sin_ref, xi_ref, dec_ref, gn_ref,
                       og_ref, rn_ref, qpad, kpad, o_sc, *, n_heads, half, scale):
    @pl.when(pl.program_id(0) == 0)
    def _():
        qpad[...] = jnp.zeros(qpad.shape, F32)
        kpad[...] = jnp.zeros(kpad.shape, F32)

    cos, sin = cos_ref[...], sin_ref[...]
    q = _rope(q_ref[0], cos, sin, half)
    k = _rope(k_ref[0], cos, sin, half) * scale
    qpad[0:n_heads, :] = q
    kpad[0:n_heads, :] = k
    q_t = qpad[...].T
    k_t = kpad[...].T
    qk = jnp.sum(q * k, axis=-1, keepdims=True)
    v = v_ref[0]
    for h in range(n_heads):
        rh = st_ref[0, h]
        vh = v[h:h + 1, :]
        qr = jnp.sum(rh * q_t[:, h:h + 1], axis=0, keepdims=True)
        o_sc[h:h + 1, :] = qk[h:h + 1, :] * vh + xi_ref[h:h + 1, :] * qr
        rn_ref[0, h] = rh * dec_ref[h:h + 1, :] + k_t[:, h:h + 1] * vh
    og_ref[0] = _group_norm_gate(o_sc[...], gn_ref[...], g_ref[0]).astype(BF16)


def _ret_sample(proj, state, j, prev_new_state, gn_w, n_heads, dk, dv):
    _, bs, n6 = proj.shape
    half = dk // 2
    cos, sin = _rope_tables(np.full((1,), PAST_LEN), half)
    gamma = np.ascontiguousarray(np.broadcast_to(np.exp(_log_gamma(n_heads))[:, None], (n_heads, dv)))
    qk_view = proj.reshape(bs, n6 // dk, dk)
    vg_view = proj.reshape(bs, n6 // dv, dv)
    in_specs = [pl.BlockSpec((1, n_heads, dk), lambda b: (b, 0, 0)),
                pl.BlockSpec((1, n_heads, dk), lambda b: (b, 1, 0)),
                pl.BlockSpec((1, n_heads, dv), lambda b: (b, 1, 0)),
                pl.BlockSpec((1, n_heads, dv), lambda b: (b, 2, 0)),
                pl.BlockSpec((None, 1, n_heads, dk, dv), lambda b: (j, b, 0, 0, 0)),
                pl.BlockSpec((1, half), lambda b: (0, 0)),
                pl.BlockSpec((1, half), lambda b: (0, 0)),
                pl.BlockSpec((n_heads, dv), lambda b: (0, 0)),
                pl.BlockSpec((n_heads, dv), lambda b: (0, 0)),
                pl.BlockSpec((n_heads, dv), lambda b: (0, 0))]
    args = [qk_view, qk_view, vg_view, vg_view, state, cos, sin, gamma, gamma, gn_w.reshape(n_heads, dv)]
    aliases = {}
    kern = functools.partial(_ret_sample_kernel, n_heads=n_heads, half=half, scale=dk ** -0.5)
    if prev_new_state is not None:
        in_specs.append(pl.BlockSpec(memory_space=pl.ANY))
        args.append(prev_new_state)
        aliases = {len(args) - 1: 1}
        inner = kern
        kern = lambda *refs: inner(*refs[:10], *refs[11:])
    og, new_state = pl.pallas_call(
        kern,
        grid=(bs,),
        in_specs=in_specs,
        out_specs=[pl.BlockSpec((1, n_heads, dv), lambda b: (b, 0, 0)),
                   pl.BlockSpec((None, 1, n_heads, dk, dv), lambda b: (j, b, 0, 0, 0))],
        out_shape=[jax.ShapeDtypeStruct((bs, n_heads, dv), BF16),
                   jax.ShapeDtypeStruct(state.shape, state.dtype)],
        scratch_shapes=[pltpu.VMEM((LANES, dk), F32), pltpu.VMEM((LANES, dk), F32),
                        pltpu.VMEM((n_heads, dv), F32)],
        input_output_aliases=aliases,
        compiler_params=_params(1),
        name="retention_sample",
    )(*args)
    return og.reshape(1, bs, n_heads * dv), new_state


def _layer_norm_silu(y, w, b):
    mu = jnp.mean(y, axis=-1, keepdims=True)
    d = y - mu
    var = jnp.mean(d * d, axis=-1, keepdims=True)
    return _silu(d * lax.rsqrt(var + NORM_EPS) * w + b)


def _conv_prompt_kernel(u_ref, wdw_ref, bdw_ref, lnw_ref, lnb_ref, z_ref, nc_ref, ext, zs, ys, *, width, tl, pad):
    li = pl.program_id(1)

    @pl.when(li == 0)
    def _():
        ext[0:pad, :] = jnp.zeros((pad, ext.shape[1]), F32)

    @pl.when(li > 0)
    def _():
        ext[0:pad, :] = ext[tl:tl + pad, :]

    ext[pad:pad + tl, :] = u_ref[0]
    off = pad - (width - 1)
    for s0 in range(0, ext.shape[1], LANES):
        cols = slice(s0, s0 + LANES)
        y = None
        for b in range(8):
            rows = tl if b == 0 else tl + 8
            acc = None
            for c in range(off, off + width):
                if c % 8 == b:
                    term = wdw_ref[c - off:c - off + 1, cols] * ext[c - b:c - b + rows, cols]
                    acc = term if acc is None else acc + term
            if b == 0:
                y = bdw_ref[:, cols] + acc
            else:
                zs[0:rows, cols] = acc
                y = y + zs[b:b + tl, cols]
        ys[:, cols] = y
    z_ref[0] = _layer_norm_silu(ys[...], lnw_ref[...], lnb_ref[...]).astype(BF16)

    @pl.when(li == pl.num_programs(1) - 1)
    def _():
        nc_ref[0] = ext[tl + off:tl + pad, :]


def _conv_prompt(u, w_dw, b_dw, ln_w, ln_b, j):
    b, l, d = u.shape
    width = w_dw.shape[1]
    tl = min(l, 128)
    pad = 32
    vec = lambda a: a[:, None, :]
    return pl.pallas_call(
        functools.partial(_conv_prompt_kernel, width=width, tl=tl, pad=pad),
        grid=(b, l // tl),
        in_specs=[pl.BlockSpec((1, tl, d), lambda bi, li: (bi, li, 0)),
                  pl.BlockSpec((None, width, d), lambda bi, li: (j, 0, 0)),
                  pl.BlockSpec((None, 1, d), lambda bi, li: (j, 0, 0)),
                  pl.BlockSpec((None, 1, d), lambda bi, li: (j, 0, 0)),
                  pl.BlockSpec((None, 1, d), lambda bi, li: (j, 0, 0))],
        out_specs=[pl.BlockSpec((1, tl, d), lambda bi, li: (bi, li, 0)),
                   pl.BlockSpec((1, width - 1, d), lambda bi, li: (bi, 0, 0))],
        out_shape=[jax.ShapeDtypeStruct((b, l, d), BF16),
                   jax.ShapeDtypeStruct((b, width - 1, d), F32)],
        scratch_shapes=[pltpu.VMEM((tl + pad, d), F32), pltpu.VMEM((tl + 8, d), F32), pltpu.VMEM((tl, d), F32)],
        compiler_params=_params(2),
        name="conv_prompt",
    )(u, w_dw, vec(b_dw), vec(ln_w), vec(ln_b))


def _conv_sample_kernel(st_ref, u_ref, wdw_ref, bdw_ref, lnw_ref, lnb_ref, z_ref, nc_ref, *, width):
    st = st_ref[...]
    u = u_ref[...]
    y = (jnp.sum(st * wdw_ref[0:width - 1, :], axis=0, keepdims=True)
         + u * wdw_ref[width - 1:width, :] + bdw_ref[...])
    z_ref[...] = _layer_norm_silu(y, lnw_ref[...], lnb_ref[...]).astype(BF16)
    nc_ref[0:width - 2, :] = st[1:width - 1, :]
    nc_ref[width - 2:width - 1, :] = u


def _conv_sample(u, state, w_dw, b_dw, ln_w, ln_b, j):
    _, bs, d = u.shape
    width = w_dw.shape[1]
    vec = lambda a: a[:, None, :]
    z, nc = pl.pallas_call(
        functools.partial(_conv_sample_kernel, width=width),
        grid=(bs,),
        in_specs=[pl.BlockSpec((None, None, width - 1, d), lambda b: (j, b, 0, 0)),
                  pl.BlockSpec((None, 1, d), lambda b: (b, 0, 0)),
                  pl.BlockSpec((None, width, d), lambda b: (j, 0, 0)),
                  pl.BlockSpec((None, 1, d), lambda b: (j, 0, 0)),
                  pl.BlockSpec((None, 1, d), lambda b: (j, 0, 0)),
                  pl.BlockSpec((None, 1, d), lambda b: (j, 0, 0))],
        out_specs=[pl.BlockSpec((None, 1, d), lambda b: (b, 0, 0)),
                   pl.BlockSpec((None, width - 1, d), lambda b: (b, 0, 0))],
        out_shape=[jax.ShapeDtypeStruct((bs, 1, d), BF16),
                   jax.ShapeDtypeStruct((bs, width - 1, d), state.dtype)],
        compiler_params=_params(1),
        name="conv_sample",
    )(state, u.reshape(bs, 1, d), w_dw, vec(b_dw), vec(ln_w), vec(ln_b))
    return z.reshape(1, bs, d), nc


def _sort_network(n):
    pairs = []
    p = 1
    while p < n:
        k = p
        while k >= 1:
            for j in range(k % p, n - k, 2 * k):
                for i in range(min(k, n - j - k)):
                    if (i + j) // (2 * p) == (i + j + k) // (2 * p):
                        pairs.append((i + j, i + j + k))
            k //= 2
        p *= 2
    return pairs


def _bitonic_merge(n):
    pairs = []
    k = n // 2
    while k >= 1:
        pairs += [(i, i + k) for i in range(n) if i & k == 0]
        k //= 2
    return pairs


def _compare_exchange(v, pairs):
    v = list(v)
    for i, j in pairs:
        v[i], v[j] = jnp.maximum(v[i], v[j]), jnp.minimum(v[i], v[j])
    return v


def _top_rows(s, count):
    n = s.shape[0] // 8
    assert n == count and n & (n - 1) == 0
    v = [s[8 * g:8 * (g + 1), :] for g in range(n)]
    v = _compare_exchange(v, _sort_network(n))
    shift = 4
    while shift >= 1:
        w = [pltpu.roll(x, shift, axis=0) for x in v]
        v = _compare_exchange([jnp.maximum(v[i], w[n - 1 - i]) for i in range(n)], _bitonic_merge(n))
        shift //= 2
    return [x[0:1, :] for x in v]


def _stack_rows(rows, base):
    sub = lax.broadcasted_iota(jnp.int32, (8, rows[0].shape[1]), 0)
    out = jnp.broadcast_to(rows[base + 7], sub.shape)
    for r in range(6, -1, -1):
        out = jnp.where(sub == r, rows[base + r], out)
    return out


def _candidates(a, b, b_lo, b_hi, topk):
    assert topk == 16
    sub = lax.broadcasted_iota(jnp.int32, (8, a[0].shape[1]), 0)
    a_hi = _stack_rows(a, 8)
    parts = [a[0] + b_lo, a[0] + b_hi]
    for r1 in range(1, 8):
        n = topk // (r1 + 1)
        c = a[r1] + b_lo
        parts.append(c if n >= 8 else jnp.where(sub < n, c, -jnp.inf))
    parts.append(a_hi + b[0])
    return parts


def _kth_largest(parts, k):
    parts = list(parts)
    for _ in range(k):
        m = parts[0]
        for p in parts[1:]:
            m = jnp.maximum(m, p)
        m = jnp.max(m, axis=0, keepdims=True)
        parts = [jnp.where(p == m, -jnp.inf, p) for p in parts]
    return m


def _route_kernel(h_ref, wq_ref, keys_ref, s1_ref, s2_ref, e1_ref, e2_ref, thr_ref, bl_ref, s_sc, *, half, topk, tr):
    q = jnp.dot(h_ref[0], wq_ref[...].astype(BF16), preferred_element_type=F32).astype(BF16)
    s_sc[0] = lax.dot_general(keys_ref[0, 0].astype(BF16), q[:, :half], _NT, preferred_element_type=F32)
    s_sc[1] = lax.dot_general(keys_ref[0, 1].astype(BF16), q[:, half:], _NT, preferred_element_type=F32)
    for lt in range(tr // LANES):
        lanes = slice(lt * LANES, (lt + 1) * LANES)
        s1 = s_sc[0, :, lanes]
        s2 = s_sc[1, :, lanes]
        a = _top_rows(s1, topk)
        b = _top_rows(s2, topk)
        b_lo, b_hi = _stack_rows(b, 0), _stack_rows(b, 8)
        parts = _candidates(a, b, b_lo, b_hi, topk)
        thr = _kth_largest(parts, topk)
        cmax = a[0] + b[0]
        z = jnp.zeros_like(thr)
        for c in parts:
            z = z + jnp.sum(jnp.where(c >= thr, jnp.exp(c - cmax), 0.0), axis=0, keepdims=True)
        s1_ref[0, 0, lt] = s1
        s2_ref[0, 0, lt] = s2
        e1_ref[0, 0, lt] = jnp.exp(s1 - a[0]) * (1.0 / z)
        e2_ref[0, 0, lt] = jnp.exp(s2 - b[0]).astype(BF16)
        thr_ref[0, 0, lt] = thr
        bl_ref[0, 0, lt, 0:8, :] = b_lo
        bl_ref[0, 0, lt, 8:16, :] = b_hi


def _route(h, w_q, keys, layer):
    g, r, d = h.shape
    _, hp, _, nk, half = keys.shape
    dq = 2 * half
    tr = min(r, ROUTE_TILE)
    nlt = tr // LANES
    tab = jax.ShapeDtypeStruct((g, hp, r // LANES, nk, LANES), F32)
    tab_spec = pl.BlockSpec((1, 1, nlt, nk, LANES), lambda gi, ti, hi: (gi, hi, ti, 0, 0))
    return pl.pallas_call(
        functools.partial(_route_kernel, half=half, topk=PEER_TOPK, tr=tr),
        grid=(g, r // tr, hp),
        in_specs=[pl.BlockSpec((1, tr, d), lambda gi, ti, hi: (gi, ti, 0)),
                  pl.BlockSpec((None, d, dq), lambda gi, ti, hi: (layer, 0, hi)),
                  pl.BlockSpec((None, 1, 2, nk, half), lambda gi, ti, hi: (layer, hi, 0, 0, 0))],
        out_specs=[tab_spec, tab_spec, tab_spec, tab_spec,
                   pl.BlockSpec((1, 1, nlt, 1, LANES), lambda gi, ti, hi: (gi, hi, ti, 0, 0)),
                   pl.BlockSpec((1, 1, nlt, PEER_TOPK, LANES), lambda gi, ti, hi: (gi, hi, ti, 0, 0))],
        out_shape=[tab, tab, tab, jax.ShapeDtypeStruct(tab.shape, BF16),
                   jax.ShapeDtypeStruct((g, hp, r // LANES, 1, LANES), F32),
                   jax.ShapeDtypeStruct((g, hp, r // LANES, PEER_TOPK, LANES), F32)],
        scratch_shapes=[pltpu.VMEM((2, nk, tr), F32)],
        compiler_params=_params(3),
        name="peer_route",
    )(h, w_q, keys)


_ERF_NUM = (-2.72614225801306e-10, 2.77068142495902e-08, -2.10102402082508e-06, -5.69250639462346e-05,
            -7.34990630326855e-04, -2.95459980854025e-03, -1.60960333262415e-02)
_ERF_DEN = (-1.45660718464996e-05, -2.13374055278905e-04, -1.68282697438203e-03, -7.37332916720468e-03,
            -1.42647390514189e-02)


_GELU_NUM = tuple(c * 0.5 ** (len(_ERF_NUM) - 1 - m) * 0.5 * 2.0 ** -0.5 for m, c in enumerate(_ERF_NUM))
_GELU_DEN = tuple(c * 0.5 ** (len(_ERF_DEN) - 1 - m) for m, c in enumerate(_ERF_DEN))
_GELU_CLAMP = 4.0 * 2.0 ** 0.5


def _gelu(a):
    ac = jnp.clip(a, -_GELU_CLAMP, _GELU_CLAMP)
    a2 = ac * ac
    p = _GELU_NUM[0] * a2 + _GELU_NUM[1]
    for c in _GELU_NUM[2:]:
        p = p * a2 + c
    q = _GELU_DEN[0] * a2 + _GELU_DEN[1]
    for c in _GELU_DEN[2:]:
        q = q * a2 + c
    return a * (0.5 + ac * p / q)


def _peer_dense_kernel(h_ref, u_ref, v_ref, s1_ref, s2_ref, e1_ref, e2_ref, thr_ref, bl_ref, x_ref, gate_ref,
                       o_ref, a_sc, w_sc, *, n_heads, nk, tt, eb):
    e = pl.program_id(2)

    @pl.when(e == 0)
    def _():
        o_ref[0] = jnp.zeros(o_ref.shape[1:], F32)

    a_sc[...] = lax.dot_general(u_ref[...], h_ref[0], _NT, preferred_element_type=F32)

    rows_per = 16
    tile = (rows_per, LANES)
    for lt in range(tt // LANES):
        lanes = slice(lt * LANES, (lt + 1) * LANES)
        t_rows = []
        for h in range(n_heads):
            s1_blk = s1_ref[0, h, lt]
            thr_row = thr_ref[0, h, lt]
            t = jnp.full(s1_blk.shape, jnp.inf, F32)
            for r in range(bl_ref.shape[3]):
                bv = bl_ref[0, h, lt, r:r + 1, :]
                t = jnp.where(s1_blk + bv >= thr_row, bv, t)
            t_rows.append(t)
        for ii in range(eb // nk):
            t_b = [jnp.broadcast_to(t_rows[h][ii:ii + 1, :], tile) for h in range(n_heads)]
            e1_b = [jnp.broadcast_to(e1_ref[0, h, lt, ii:ii + 1, :], tile).astype(BF16) for h in range(n_heads)]
            for jc in range(nk // rows_per):
                jrows = slice(jc * rows_per, (jc + 1) * rows_per)
                rows = slice(ii * nk + jc * rows_per, ii * nk + (jc + 1) * rows_per)
                gsum = jnp.zeros(tile, BF16)
                for h in range(n_heads):
                    hit = s2_ref[0, h, lt, jrows, :] >= t_b[h]
                    gsum = gsum + jnp.where(hit, e1_b[h] * e2_ref[0, h, lt, jrows, :], jnp.zeros(tile, BF16))
                w_sc[rows, lanes] = (gsum.astype(F32) * _gelu(a_sc[rows, lanes])).astype(BF16)

    o_ref[0] += lax.dot_general(w_sc[...], v_ref[...], _TN, preferred_element_type=F32)

    @pl.when(e == pl.num_programs(2) - 1)
    def _():
        o_ref[0] = x_ref[0] + gate_ref[0] * o_ref[0]


def _peer_dense(h, u, v, layer, tables, x, mod, gate_chunk):
    g, r, d = h.shape
    s1, s2, e1, e2, thr, bl = tables
    hp, nk = s1.shape[1], s1.shape[3]
    tt = min(r, ROW_TILE)
    eb = EXPERT_TILE
    per = eb // nk
    assert tt % LANES == 0 and per == 8
    nlt = tt // LANES
    full_spec = pl.BlockSpec((1, hp, nlt, nk, LANES), lambda gi, ti, ei: (gi, 0, ti, 0, 0))
    rows_spec = pl.BlockSpec((1, hp, nlt, per, LANES), lambda gi, ti, ei: (gi, 0, ti, ei, 0))
    return pl.pallas_call(
        functools.partial(_peer_dense_kernel, n_heads=hp, nk=nk, tt=tt, eb=eb),
        grid=(g, r // tt, u.shape[1] // eb),
        in_specs=[pl.BlockSpec((1, tt, d), lambda gi, ti, ei: (gi, ti, 0)),
                  pl.BlockSpec((None, eb, d), lambda gi, ti, ei: (layer, ei, 0)),
                  pl.BlockSpec((None, eb, d), lambda gi, ti, ei: (layer, ei, 0)),
                  rows_spec, full_spec, rows_spec, full_spec,
                  pl.BlockSpec((1, hp, nlt, 1, LANES), lambda gi, ti, ei: (gi, 0, ti, 0, 0)),
                  pl.BlockSpec((1, hp, nlt, bl.shape[3], LANES), lambda gi, ti, ei: (gi, 0, ti, 0, 0)),
                  pl.BlockSpec((1, tt, d), lambda gi, ti, ei: (gi, ti, 0)),
                  _mod_spec(mod, tt, gate_chunk, lambda gi, ti, ei: (gi, ti))],
        out_specs=pl.BlockSpec((1, tt, d), lambda gi, ti, ei: (gi, ti, 0)),
        out_shape=jax.ShapeDtypeStruct((g, r, d), F32),
        scratch_shapes=[pltpu.VMEM((eb, tt), F32), pltpu.VMEM((eb, tt), BF16)],
        compiler_params=_params(3),
        name="peer_dense",
    )(h, u, v, s1, s2, e1, e2, thr, bl, x, mod)


def _trunk(x, mods, is_prompt, state_ret, state_conv, norm_w, final_norm_w,
           ret_w_in, ret_gn_w, ret_w_out, conv_w_pw1, conv_b_pw1, conv_w_dw, conv_b_dw,
           conv_ln_w, conv_ln_b, conv_w_pw2, conv_b_pw2, peer_w_q, peer_keys, peer_u, peer_v):
    depth = norm_w.shape[0]
    n_heads, dk, dv = state_ret.shape[2:]
    new_ret, new_conv = None, []
    ret_prompt_states = []
    vec = lambda a: a[:, None, :]
    for i in range(depth):
        mod = mods[i]
        h = _modnorm(x, norm_w[i, 0], mod, 0, 1)
        j = i // 2
        if i % 2 == 0:
            proj = _mm_plain(h, ret_w_in, j)
            if is_prompt:
                og, r = _ret_prompt(proj, ret_gn_w[j], n_heads, dk, dv)
                ret_prompt_states.append(r)
            else:
                og, new_ret = _ret_sample(proj, state_ret, j, new_ret, ret_gn_w[j], n_heads, dk, dv)
            x = _mm_resid(og, ret_w_out, None, j, x, mod, 2)
        else:
            u = _mm_glu(h, conv_w_pw1, vec(conv_b_pw1), j)
            if is_prompt:
                z, nc = _conv_prompt(u, conv_w_dw, conv_b_dw, conv_ln_w, conv_ln_b, j)
            else:
                z, nc = _conv_sample(u, state_conv, conv_w_dw, conv_b_dw, conv_ln_w, conv_ln_b, j)
            new_conv.append(nc)
            x = _mm_resid(z, conv_w_pw2, vec(conv_b_pw2), j, x, mod, 2)
        h = _modnorm(x, norm_w[i, 1], mod, 3, 4)
        tables = _route(h, peer_w_q, peer_keys, i)
        x = _peer_dense(h, peer_u, peer_v, i, tables, x, mod, 5)
    y = _rmsnorm(x, final_norm_w)
    if is_prompt:
        new_ret = jnp.stack(ret_prompt_states)
    return y, new_ret, jnp.stack(new_conv)


def kernel(x_prompt, x_sample, state_ret, state_conv, c_prompt, c_sample, ada_w, ada_b, norm_w, final_norm_w,
           ret_w_in, ret_gn_w, ret_w_out, conv_w_pw1, conv_b_pw1, conv_w_dw, conv_b_dw, conv_ln_w, conv_ln_b,
           conv_w_pw2, conv_b_pw2, peer_w_q, peer_keys, peer_u, peer_v):
    b, l, d = x_prompt.shape
    bs, ls, _ = x_sample.shape
    assert ls == 1, "the sample path handles one new token per sequence"
    depth = ada_w.shape[0]

    n_c = b + bs
    n_pad = -n_c % 8
    c_all = jnp.concatenate([c_prompt, c_sample, jnp.zeros((n_pad, d), c_prompt.dtype)], axis=0)
    mod = _ada(c_all, ada_w, ada_b)
    mods_p = [mod[i, :b].reshape(b, 1, 6 * d) for i in range(depth)]
    mods_s = [mod[i, b:b + bs].reshape(1, bs, 6 * d) for i in range(depth)]

    weights = (norm_w, final_norm_w, ret_w_in, ret_gn_w, ret_w_out, conv_w_pw1, conv_b_pw1, conv_w_dw, conv_b_dw,
               conv_ln_w, conv_ln_b, conv_w_pw2, conv_b_pw2, peer_w_q, peer_keys,
               peer_u.astype(BF16), peer_v.astype(BF16))
    y_p, ret_p, conv_p = _trunk(x_prompt, mods_p, True, state_ret, state_conv, *weights)
    y_s, ret_s, conv_s = _trunk(x_sample.reshape(1, bs, d), mods_s, False, state_ret, state_conv, *weights)
    return (y_p, y_s.reshape(bs, 1, d), ret_p, conv_p, ret_s, conv_s)
```

```python
import functools
import math

import jax
import jax.numpy as jnp
import numpy as np
from jax import lax
from jax.experimental import pallas as pl
from jax.experimental.pallas import tpu as pltpu

F32 = jnp.float32
BF16 = jnp.bfloat16

NORM_EPS = 1e-6
ROPE_BASE = 10000.0
RET_CHUNK = 128
PEER_TOPK = 16
PAST_LEN = 16384

LANES = 128
VMEM_LIMIT_BYTES = 56 * 1024 * 1024
ROW_TILE = 512
MM_ROW_TILE = 1024
COL_TILE = 512
EXPERT_TILE = 1024
ROUTE_TILE = 512

_NT = (((1,), (1,)), ((), ()))
_TN = (((0,), (0,)), ((), ()))


def _params(n_axes):
    return pltpu.CompilerParams(dimension_semantics=("arbitrary",) * n_axes,
                                vmem_limit_bytes=VMEM_LIMIT_BYTES)


def _silu(x):
    return x * (1.0 / (1.0 + jnp.exp(-x)))


def _mod_spec(mod, tile, chunk, index_of):
    rm = mod.shape[1]
    d = mod.shape[2] // 6
    if rm == 1:
        return pl.BlockSpec((1, 1, d), lambda *ids: (index_of(*ids)[0], 0, chunk))
    return pl.BlockSpec((1, tile, d), lambda *ids: (index_of(*ids)[0], index_of(*ids)[1], chunk))


def _ada_kernel(c_ref, w_ref, b_ref, o_ref):
    sc = _silu(c_ref[...]).astype(BF16)
    o_ref[...] = jnp.dot(sc, w_ref[...].astype(BF16), preferred_element_type=F32) + b_ref[...]


def _ada(c_all, ada_w, ada_b):
    m, d = c_all.shape
    nl, _, n = ada_w.shape
    tn = min(n, 1024)
    return pl.pallas_call(
        _ada_kernel,
        grid=(nl, n // tn),
        in_specs=[pl.BlockSpec((m, d), lambda l, j: (0, 0)),
                  pl.BlockSpec((None, d, tn), lambda l, j: (l, 0, j)),
                  pl.BlockSpec((None, 1, tn), lambda l, j: (l, 0, j))],
        out_specs=pl.BlockSpec((None, m, tn), lambda l, j: (l, 0, j)),
        out_shape=jax.ShapeDtypeStruct((nl, m, n), F32),
        compiler_params=_params(2),
        name="ada_mod",
    )(c_all, ada_w, ada_b[:, None, :])


def _modnorm_kernel(x_ref, w_ref, sh_ref, sc_ref, o_ref):
    x = x_ref[0]
    y = x * lax.rsqrt(jnp.mean(x * x, axis=-1, keepdims=True) + NORM_EPS) * w_ref[...]
    o_ref[0] = (y * (1.0 + sc_ref[0]) + sh_ref[0]).astype(o_ref.dtype)


def _modnorm(x, w, mod, shift_chunk, scale_chunk):
    g, r, d = x.shape
    tr = min(r, ROW_TILE)
    ix = lambda gi, ti: (gi, ti)
    return pl.pallas_call(
        _modnorm_kernel,
        grid=(g, r // tr),
        in_specs=[pl.BlockSpec((1, tr, d), lambda gi, ti: (gi, ti, 0)),
                  pl.BlockSpec((1, d), lambda gi, ti: (0, 0)),
                  _mod_spec(mod, tr, shift_chunk, ix),
                  _mod_spec(mod, tr, scale_chunk, ix)],
        out_specs=pl.BlockSpec((1, tr, d), lambda gi, ti: (gi, ti, 0)),
        out_shape=jax.ShapeDtypeStruct((g, r, d), BF16),
        compiler_params=_params(2),
        name="modnorm",
    )(x, w[None, :], mod, mod)


def _rmsnorm_kernel(x_ref, w_ref, o_ref):
    x = x_ref[0]
    o_ref[0] = x * lax.rsqrt(jnp.mean(x * x, axis=-1, keepdims=True) + NORM_EPS) * w_ref[...]


def _rmsnorm(x, w):
    g, r, d = x.shape
    tr = min(r, ROW_TILE)
    return pl.pallas_call(
        _rmsnorm_kernel,
        grid=(g, r // tr),
        in_specs=[pl.BlockSpec((1, tr, d), lambda gi, ti: (gi, ti, 0)),
                  pl.BlockSpec((1, d), lambda gi, ti: (0, 0))],
        out_specs=pl.BlockSpec((1, tr, d), lambda gi, ti: (gi, ti, 0)),
        out_shape=jax.ShapeDtypeStruct((g, r, d), F32),
        compiler_params=_params(2),
        name="final_rmsnorm",
    )(x, w[None, :])


def _first_inner_step():
    return jnp.logical_and(pl.program_id(1) == 0, pl.program_id(2) == 0)


def _mm_plain_kernel(x_ref, w_ref, o_ref, wbf):
    @pl.when(_first_inner_step())
    def _():
        wbf[...] = w_ref[...].astype(BF16)
    o_ref[0] = jnp.dot(x_ref[0], wbf[...], preferred_element_type=F32)


def _mm_plain(x, w3, layer):
    g, r, k = x.shape
    n = w3.shape[2]
    tm, tn = min(r, MM_ROW_TILE), min(n, 2 * COL_TILE)
    return pl.pallas_call(
        _mm_plain_kernel,
        grid=(n // tn, g, r // tm),
        in_specs=[pl.BlockSpec((1, tm, k), lambda j, gi, ti: (gi, ti, 0)),
                  pl.BlockSpec((None, k, tn), lambda j, gi, ti: (layer, 0, j))],
        out_specs=pl.BlockSpec((1, tm, tn), lambda j, gi, ti: (gi, ti, j)),
        out_shape=jax.ShapeDtypeStruct((g, r, n), F32),
        scratch_shapes=[pltpu.VMEM((k, tn), BF16)],
        compiler_params=_params(3),
        name="mm_plain",
    )(x, w3)


def _mm_glu_kernel(x_ref, wa_ref, wb_ref, ba_ref, bb_ref, o_ref, wabf, wbbf):
    @pl.when(_first_inner_step())
    def _():
        wabf[...] = wa_ref[...].astype(BF16)
        wbbf[...] = wb_ref[...].astype(BF16)
    x = x_ref[0]
    a = jnp.dot(x, wabf[...], preferred_element_type=F32) + ba_ref[...]
    b = jnp.dot(x, wbbf[...], preferred_element_type=F32) + bb_ref[...]
    o_ref[0] = a * (1.0 / (1.0 + jnp.exp(-b)))


def _mm_glu(x, w3, b3, layer):
    g, r, k = x.shape
    n = w3.shape[2] // 2
    tm, tn = min(r, MM_ROW_TILE), min(n, COL_TILE)
    nb = n // tn
    return pl.pallas_call(
        _mm_glu_kernel,
        grid=(nb, g, r // tm),
        in_specs=[pl.BlockSpec((1, tm, k), lambda j, gi, ti: (gi, ti, 0)),
                  pl.BlockSpec((None, k, tn), lambda j, gi, ti: (layer, 0, j)),
                  pl.BlockSpec((None, k, tn), lambda j, gi, ti: (layer, 0, nb + j)),
                  pl.BlockSpec((None, 1, tn), lambda j, gi, ti: (layer, 0, j)),
                  pl.BlockSpec((None, 1, tn), lambda j, gi, ti: (layer, 0, nb + j))],
        out_specs=pl.BlockSpec((1, tm, tn), lambda j, gi, ti: (gi, ti, j)),
        out_shape=jax.ShapeDtypeStruct((g, r, n), F32),
        scratch_shapes=[pltpu.VMEM((k, tn), BF16), pltpu.VMEM((k, tn), BF16)],
        compiler_params=_params(3),
        name="mm_glu",
    )(x, w3, w3, b3, b3)


def _mm_resid_kernel(x_ref, w_ref, *rest, has_bias):
    if has_bias:
        b_ref, res_ref, gate_ref, o_ref, wbf = rest
    else:
        res_ref, gate_ref, o_ref, wbf = rest

    @pl.when(_first_inner_step())
    def _():
        wbf[...] = w_ref[...].astype(BF16)
    y = jnp.dot(x_ref[0], wbf[...], preferred_element_type=F32)
    if has_bias:
        y = y + b_ref[...]
    o_ref[0] = res_ref[0] + gate_ref[0] * y


def _mm_resid(x, w3, b3, layer, res, mod, gate_chunk):
    g, r, k = x.shape
    n = w3.shape[2]
    tm, tn = min(r, MM_ROW_TILE), min(n, COL_TILE)
    per_d = n // tn
    rm = mod.shape[1]
    if rm == 1:
        gate_spec = pl.BlockSpec((1, 1, tn), lambda j, gi, ti: (gi, 0, gate_chunk * per_d + j))
    else:
        gate_spec = pl.BlockSpec((1, tm, tn), lambda j, gi, ti: (gi, ti, gate_chunk * per_d + j))
    in_specs = [pl.BlockSpec((1, tm, k), lambda j, gi, ti: (gi, ti, 0)),
                pl.BlockSpec((None, k, tn), lambda j, gi, ti: (layer, 0, j))]
    args = [x, w3]
    if b3 is not None:
        in_specs.append(pl.BlockSpec((None, 1, tn), lambda j, gi, ti: (layer, 0, j)))
        args.append(b3)
    in_specs += [pl.BlockSpec((1, tm, tn), lambda j, gi, ti: (gi, ti, j)), gate_spec]
    args += [res, mod]
    return pl.pallas_call(
        functools.partial(_mm_resid_kernel, has_bias=b3 is not None),
        grid=(n // tn, g, r // tm),
        in_specs=in_specs,
        out_specs=pl.BlockSpec((1, tm, tn), lambda j, gi, ti: (gi, ti, j)),
        out_shape=jax.ShapeDtypeStruct((g, r, n), F32),
        scratch_shapes=[pltpu.VMEM((k, tn), BF16)],
        compiler_params=_params(3),
        name="mm_resid",
    )(*args)


def _rope(x, cos, sin, half):
    x1, x2 = x[:, :half], x[:, half:]
    return jnp.concatenate([x1 * cos - x2 * sin, x1 * sin + x2 * cos], axis=-1)


def _group_norm_gate(o, gn_w, g):
    mu = jnp.mean(o, axis=-1, keepdims=True)
    d = o - mu
    var = jnp.mean(d * d, axis=-1, keepdims=True)
    return d * lax.rsqrt(var + NORM_EPS) * gn_w * _silu(g)


def _ret_prompt_kernel(q_ref, k_ref, v_ref, g_ref, cos_ref, sin_ref, dm_ref, xi_ref, zt_ref, dec_ref, gn_ref,
                       og_ref, r_ref, *, n_heads, dk, dv, scale):
    @pl.when(pl.program_id(1) == 0)
    def _():
        r_ref[0] = jnp.zeros(r_ref.shape[1:], F32)

    cos, sin = cos_ref[...], sin_ref[...]
    for h in range(n_heads):
        kcols = slice(h * dk, (h + 1) * dk)
        vcols = slice(h * dv, (h + 1) * dv)
        q = _rope(q_ref[0, :, kcols], cos, sin, dk // 2)
        k = _rope(k_ref[0, :, kcols], cos, sin, dk // 2) * scale
        vb = v_ref[0, :, vcols].astype(BF16)
        r = r_ref[0, h]
        sc = lax.dot_general(q.astype(BF16), k.astype(BF16), _NT, preferred_element_type=F32) * dm_ref[h]
        o = (jnp.dot(sc.astype(BF16), vb, preferred_element_type=F32)
             + jnp.dot((q * xi_ref[h]).astype(BF16), r.astype(BF16), preferred_element_type=F32))
        r_ref[0, h] = r * dec_ref[h] + lax.dot_general((k * zt_ref[h]).astype(BF16), vb, _TN,
                                                       preferred_element_type=F32)
        og_ref[0, :, vcols] = _group_norm_gate(o, gn_ref[:, vcols], g_ref[0, :, vcols]).astype(BF16)


def _log_gamma(n_heads):
    return np.log(np.float32(1.0) - np.float32(2.0) ** (np.float32(-5.0) - np.arange(n_heads, dtype=np.float32)))


def _retention_tables(n_heads, chunk, dk, dv):
    lg = _log_gamma(n_heads)
    idx = np.arange(chunk, dtype=np.float32)
    diff = idx[:, None] - idx[None, :]
    dmask = np.where(diff[None] >= 0, np.exp(np.maximum(diff, np.float32(0.0))[None] * lg[:, None, None]),
                     np.float32(0.0)).astype(np.float32)
    xi = np.exp((idx + np.float32(1.0))[None, :] * lg[:, None])
    zeta = np.exp((np.float32(chunk - 1.0) - idx)[None, :] * lg[:, None])
    dec = np.exp(np.float32(chunk) * lg)
    return (dmask,
            np.ascontiguousarray(np.broadcast_to(xi[:, :, None], (n_heads, chunk, dk))),
            np.ascontiguousarray(np.broadcast_to(zeta[:, :, None], (n_heads, chunk, dk))),
            np.ascontiguousarray(np.broadcast_to(dec[:, None, None], (n_heads, 1, dv))))


def _rope_tables(pos, half):
    inv = np.float32(ROPE_BASE) ** (-np.arange(half, dtype=np.float32) / np.float32(half))
    ang = pos.astype(np.float32)[:, None] * inv[None, :]
    return np.cos(ang).astype(np.float32), np.sin(ang).astype(np.float32)


def _ret_prompt(proj, gn_w, n_heads, dk, dv):
    b, l, _ = proj.shape
    d = n_heads * dk
    chunk = math.gcd(l, RET_CHUNK)
    half = dk // 2
    cos, sin = _rope_tables(np.arange(l), half)
    dmask, xiq, ztk, dec = _retention_tables(n_heads, chunk, dk, dv)
    dvs = n_heads * dv
    assert dvs == 2 * d
    const = lambda shape: pl.BlockSpec(shape, lambda bi, c: (0,) * len(shape))
    og, r = pl.pallas_call(
        functools.partial(_ret_prompt_kernel, n_heads=n_heads, dk=dk, dv=dv, scale=dk ** -0.5),
        grid=(b, l // chunk),
        in_specs=[pl.BlockSpec((1, chunk, d), lambda bi, c: (bi, c, 0)),
                  pl.BlockSpec((1, chunk, d), lambda bi, c: (bi, c, 1)),
                  pl.BlockSpec((1, chunk, dvs), lambda bi, c: (bi, c, 1)),
                  pl.BlockSpec((1, chunk, dvs), lambda bi, c: (bi, c, 2)),
                  pl.BlockSpec((chunk, half), lambda bi, c: (c, 0)),
                  pl.BlockSpec((chunk, half), lambda bi, c: (c, 0)),
                  const((n_heads, chunk, chunk)), const((n_heads, chunk, dk)), const((n_heads, chunk, dk)),
                  const((n_heads, 1, dv)), const((1, dvs))],
        out_specs=[pl.BlockSpec((1, chunk, dvs), lambda bi, c: (bi, c, 0)),
                   pl.BlockSpec((1, n_heads, dk, dv), lambda bi, c: (bi, 0, 0, 0))],
        out_shape=[jax.ShapeDtypeStruct((b, l, dvs), BF16),
                   jax.ShapeDtypeStruct((b, n_heads, dk, dv), F32)],
        compiler_params=_params(2),
        name="retention_prompt",
    )(proj, proj, proj, proj, cos, sin, dmask, xiq, ztk, dec, gn_w[None, :])
    return og, r


def _ret_sample_kernel(q_ref, k_ref, v_ref, g_ref, st_ref, cos_ref, sin_ref, xi_ref, dec_ref, gn_ref,
                       og_ref, rn_ref, qpad, kpad, o_sc, *, n_heads, half, scale):
    @pl.when(pl.program_id(0) == 0)
    def _():
        qpad[...] = jnp.zeros(qpad.shape, F32)
        kpad[...] = jnp.zeros(kpad.shape, F32)

    cos, sin = cos_ref[...], sin_ref[...]
    q = _rope(q_ref[0], cos, sin, half)
    k = _rope(k_ref[0], cos, sin, half) * scale
    qpad[0:n_heads, :] = q
    kpad[0:n_heads, :] = k
    q_t = qpad[...].T
    k_t = kpad[...].T
    qk = jnp.sum(q * k, axis=-1, keepdims=True)
    v = v_ref[0]
    for h in range(n_heads):
        rh = st_ref[0, h]
        vh = v[h:h + 1, :]
        qr = jnp.sum(rh * q_t[:, h:h + 1], axis=0, keepdims=True)
        o_sc[h:h + 1, :] = qk[h:h + 1, :] * vh + xi_ref[h:h + 1, :] * qr
        rn_ref[0, h] = rh * dec_ref[h:h + 1, :] + k_t[:, h:h + 1] * vh
    og_ref[0] = _group_norm_gate(o_sc[...], gn_ref[...], g_ref[0]).astype(BF16)


def _ret_sample(proj, state, j, prev_new_state, gn_w, n_heads, dk, dv):
    _, bs, n6 = proj.shape
    half = dk // 2
    cos, sin = _rope_tables(np.full((1,), PAST_LEN), half)
    gamma = np.ascontiguousarray(np.broadcast_to(np.exp(_log_gamma(n_heads))[:, None], (n_heads, dv)))
    qk_view = proj.reshape(bs, n6 // dk, dk)
    vg_view = proj.reshape(bs, n6 // dv, dv)
    in_specs = [pl.BlockSpec((1, n_heads, dk), lambda b: (b, 0, 0)),
                pl.BlockSpec((1, n_heads, dk), lambda b: (b, 1, 0)),
                pl.BlockSpec((1, n_heads, dv), lambda b: (b, 1, 0)),
                pl.BlockSpec((1, n_heads, dv), lambda b: (b, 2, 0)),
                pl.BlockSpec((None, 1, n_heads, dk, dv), lambda b: (j, b, 0, 0, 0)),
                pl.BlockSpec((1, half), lambda b: (0, 0)),
                pl.BlockSpec((1, half), lambda b: (0, 0)),
                pl.BlockSpec((n_heads, dv), lambda b: (0, 0)),
                pl.BlockSpec((n_heads, dv), lambda b: (0, 0)),
                pl.BlockSpec((n_heads, dv), lambda b: (0, 0))]
    args = [qk_view, qk_view, vg_view, vg_view, state, cos, sin, gamma, gamma, gn_w.reshape(n_heads, dv)]
    aliases = {}
    kern = functools.partial(_ret_sample_kernel, n_heads=n_heads, half=half, scale=dk ** -0.5)
    if prev_new_state is not None:
        in_specs.append(pl.BlockSpec(memory_space=pl.ANY))
        args.append(prev_new_state)
        aliases = {len(args) - 1: 1}
        inner = kern
        kern = lambda *refs: inner(*refs[:10], *refs[11:])
    og, new_state = pl.pallas_call(
        kern,
        grid=(bs,),
        in_specs=in_specs,
        out_specs=[pl.BlockSpec((1, n_heads, dv), lambda b: (b, 0, 0)),
                   pl.BlockSpec((None, 1, n_heads, dk, dv), lambda b: (j, b, 0, 0, 0))],
        out_shape=[jax.ShapeDtypeStruct((bs, n_heads, dv), BF16),
                   jax.ShapeDtypeStruct(state.shape, state.dtype)],
        scratch_shapes=[pltpu.VMEM((LANES, dk), F32), pltpu.VMEM((LANES, dk), F32),
                        pltpu.VMEM((n_heads, dv), F32)],
        input_output_aliases=aliases,
        compiler_params=_params(1),
        name="retention_sample",
    )(*args)
    return og.reshape(1, bs, n_heads * dv), new_state


def _layer_norm_silu(y, w, b):
    mu = jnp.mean(y, axis=-1, keepdims=True)
    d = y - mu
    var = jnp.mean(d * d, axis=-1, keepdims=True)
    return _silu(d * lax.rsqrt(var + NORM_EPS) * w + b)


def _conv_prompt_kernel(u_ref, wdw_ref, bdw_ref, lnw_ref, lnb_ref, z_ref, nc_ref, ext, zs, ys, *, width, tl, pad):
    li = pl.program_id(1)

    @pl.when(li == 0)
    def _():
        ext[0:pad, :] = jnp.zeros((pad, ext.shape[1]), F32)

    @pl.when(li > 0)
    def _():
        ext[0:pad, :] = ext[tl:tl + pad, :]

    ext[pad:pad + tl, :] = u_ref[0]
    off = pad - (width - 1)
    for s0 in range(0, ext.shape[1], LANES):
        cols = slice(s0, s0 + LANES)
        y = None
        for b in range(8):
            rows = tl if b == 0 else tl + 8
            acc = None
            for c in range(off, off + width):
                if c % 8 == b:
                    term = wdw_ref[c - off:c - off + 1, cols] * ext[c - b:c - b + rows, cols]
                    acc = term if acc is None else acc + term
            if b == 0:
                y = bdw_ref[:, cols] + acc
            else:
                zs[0:rows, cols] = acc
                y = y + zs[b:b + tl, cols]
        ys[:, cols] = y
    z_ref[0] = _layer_norm_silu(ys[...], lnw_ref[...], lnb_ref[...]).astype(BF16)

    @pl.when(li == pl.num_programs(1) - 1)
    def _():
        nc_ref[0] = ext[tl + off:tl + pad, :]


def _conv_prompt(u, w_dw, b_dw, ln_w, ln_b, j):
    b, l, d = u.shape
    width = w_dw.shape[1]
    tl = min(l, 128)
    pad = 32
    vec = lambda a: a[:, None, :]
    return pl.pallas_call(
        functools.partial(_conv_prompt_kernel, width=width, tl=tl, pad=pad),
        grid=(b, l // tl),
        in_specs=[pl.BlockSpec((1, tl, d), lambda bi, li: (bi, li, 0)),
                  pl.BlockSpec((None, width, d), lambda bi, li: (j, 0, 0)),
                  pl.BlockSpec((None, 1, d), lambda bi, li: (j, 0, 0)),
                  pl.BlockSpec((None, 1, d), lambda bi, li: (j, 0, 0)),
                  pl.BlockSpec((None, 1, d), lambda bi, li: (j, 0, 0))],
        out_specs=[pl.BlockSpec((1, tl, d), lambda bi, li: (bi, li, 0)),
                   pl.BlockSpec((1, width - 1, d), lambda bi, li: (bi, 0, 0))],
        out_shape=[jax.ShapeDtypeStruct((b, l, d), BF16),
                   jax.ShapeDtypeStruct((b, width - 1, d), F32)],
        scratch_shapes=[pltpu.VMEM((tl + pad, d), F32), pltpu.VMEM((tl + 8, d), F32), pltpu.VMEM((tl, d), F32)],
        compiler_params=_params(2),
        name="conv_prompt",
    )(u, w_dw, vec(b_dw), vec(ln_w), vec(ln_b))


def _conv_sample_kernel(st_ref, u_ref, wdw_ref, bdw_ref, lnw_ref, lnb_ref, z_ref, nc_ref, *, width):
    st = st_ref[...]
    u = u_ref[...]
    y = (jnp.sum(st * wdw_ref[0:width - 1, :], axis=0, keepdims=True)
         + u * wdw_ref[width - 1:width, :] + bdw_ref[...])
    z_ref[...] = _layer_norm_silu(y, lnw_ref[...], lnb_ref[...]).astype(BF16)
    nc_ref[0:width - 2, :] = st[1:width - 1, :]
    nc_ref[width - 2:width - 1, :] = u


def _conv_sample(u, state, w_dw, b_dw, ln_w, ln_b, j):
    _, bs, d = u.shape
    width = w_dw.shape[1]
    vec = lambda a: a[:, None, :]
    z, nc = pl.pallas_call(
        functools.partial(_conv_sample_kernel, width=width),
        grid=(bs,),
        in_specs=[pl.BlockSpec((None, None, width - 1, d), lambda b: (j, b, 0, 0)),
                  pl.BlockSpec((None, 1, d), lambda b: (b, 0, 0)),
                  pl.BlockSpec((None, width, d), lambda b: (j, 0, 0)),
                  pl.BlockSpec((None, 1, d), lambda b: (j, 0, 0)),
                  pl.BlockSpec((None, 1, d), lambda b: (j, 0, 0)),
                  pl.BlockSpec((None, 1, d), lambda b: (j, 0, 0))],
        out_specs=[pl.BlockSpec((None, 1, d), lambda b: (b, 0, 0)),
                   pl.BlockSpec((None, width - 1, d), lambda b: (b, 0, 0))],
        out_shape=[jax.ShapeDtypeStruct((bs, 1, d), BF16),
                   jax.ShapeDtypeStruct((bs, width - 1, d), state.dtype)],
        compiler_params=_params(1),
        name="conv_sample",
    )(state, u.reshape(bs, 1, d), w_dw, vec(b_dw), vec(ln_w), vec(ln_b))
    return z.reshape(1, bs, d), nc


def _sort_network(n):
    pairs = []
    p = 1
    while p < n:
        k = p
        while k >= 1:
            for j in range(k % p, n - k, 2 * k):
                for i in range(min(k, n - j - k)):
                    if (i + j) // (2 * p) == (i + j + k) // (2 * p):
                        pairs.append((i + j, i + j + k))
            k //= 2
        p *= 2
    return pairs


def _bitonic_merge(n):
    pairs = []
    k = n // 2
    while k >= 1:
        pairs += [(i, i + k) for i in range(n) if i & k == 0]
        k //= 2
    return pairs


def _compare_exchange(v, pairs):
    v = list(v)
    for i, j in pairs:
        v[i], v[j] = jnp.maximum(v[i], v[j]), jnp.minimum(v[i], v[j])
    return v


def _top_rows(s, count):
    n = s.shape[0] // 8
    assert n == count and n & (n - 1) == 0
    v = [s[8 * g:8 * (g + 1), :] for g in range(n)]
    v = _compare_exchange(v, _sort_network(n))
    shift = 4
    while shift >= 1:
        w = [pltpu.roll(x, shift, axis=0) for x in v]
        v = _compare_exchange([jnp.maximum(v[i], w[n - 1 - i]) for i in range(n)], _bitonic_merge(n))
        shift //= 2
    return [x[0:1, :] for x in v]


def _stack_rows(rows, base):
    sub = lax.broadcasted_iota(jnp.int32, (8, rows[0].shape[1]), 0)
    out = jnp.broadcast_to(rows[base + 7], sub.shape)
    for r in range(6, -1, -1):
        out = jnp.where(sub == r, rows[base + r], out)
    return out


def _candidates(a, b, b_lo, b_hi, topk):
    assert topk == 16
    sub = lax.broadcasted_iota(jnp.int32, (8, a[0].shape[1]), 0)
    a_hi = _stack_rows(a, 8)
    parts = [a[0] + b_lo, a[0] + b_hi]
    for r1 in range(1, 8):
        n = topk // (r1 + 1)
        c = a[r1] + b_lo
        parts.append(c if n >= 8 else jnp.where(sub < n, c, -jnp.inf))
    parts.append(a_hi + b[0])
    return parts


def _kth_largest(parts, k):
    parts = list(parts)
    for _ in range(k):
        m = parts[0]
        for p in parts[1:]:
            m = jnp.maximum(m, p)
        m = jnp.max(m, axis=0, keepdims=True)
        parts = [jnp.where(p == m, -jnp.inf, p) for p in parts]
    return m


def _route_kernel(h_ref, wq_ref, keys_ref, s1_ref, s2_ref, e1_ref, e2_ref, thr_ref, bl_ref, s_sc, *, half, topk, tr):
    q = jnp.dot(h_ref[0], wq_ref[...].astype(BF16), preferred_element_type=F32).astype(BF16)
    s_sc[0] = lax.dot_general(keys_ref[0, 0].astype(BF16), q[:, :half], _NT, preferred_element_type=F32)
    s_sc[1] = lax.dot_general(keys_ref[0, 1].astype(BF16), q[:, half:], _NT, preferred_element_type=F32)
    for lt in range(tr // LANES):
        lanes = slice(lt * LANES, (lt + 1) * LANES)
        s1 = s_sc[0, :, lanes]
        s2 = s_sc[1, :, lanes]
        a = _top_rows(s1, topk)
        b = _top_rows(s2, topk)
        b_lo, b_hi = _stack_rows(b, 0), _stack_rows(b, 8)
        parts = _candidates(a, b, b_lo, b_hi, topk)
        thr = _kth_largest(parts, topk)
        cmax = a[0] + b[0]
        z = jnp.zeros_like(thr)
        for c in parts:
            z = z + jnp.sum(jnp.where(c >= thr, jnp.exp(c - cmax), 0.0), axis=0, keepdims=True)
        s1_ref[0, 0, lt] = s1
        s2_ref[0, 0, lt] = s2
        e1_ref[0, 0, lt] = jnp.exp(s1 - a[0]) * (1.0 / z)
        e2_ref[0, 0, lt] = jnp.exp(s2 - b[0]).astype(BF16)
        thr_ref[0, 0, lt] = thr
        bl_ref[0, 0, lt, 0:8, :] = b_lo
        bl_ref[0, 0, lt, 8:16, :] = b_hi


def _route(h, w_q, keys, layer):
    g, r, d = h.shape
    _, hp, _, nk, half = keys.shape
    dq = 2 * half
    tr = min(r, ROUTE_TILE)
    nlt = tr // LANES
    tab = jax.ShapeDtypeStruct((g, hp, r // LANES, nk, LANES), F32)
    tab_spec = pl.BlockSpec((1, 1, nlt, nk, LANES), lambda gi, ti, hi: (gi, hi, ti, 0, 0))
    return pl.pallas_call(
        functools.partial(_route_kernel, half=half, topk=PEER_TOPK, tr=tr),
        grid=(g, r // tr, hp),
        in_specs=[pl.BlockSpec((1, tr, d), lambda gi, ti, hi: (gi, ti, 0)),
                  pl.BlockSpec((None, d, dq), lambda gi, ti, hi: (layer, 0, hi)),
                  pl.BlockSpec((None, 1, 2, nk, half), lambda gi, ti, hi: (layer, hi, 0, 0, 0))],
        out_specs=[tab_spec, tab_spec, tab_spec, tab_spec,
                   pl.BlockSpec((1, 1, nlt, 1, LANES), lambda gi, ti, hi: (gi, hi, ti, 0, 0)),
                   pl.BlockSpec((1, 1, nlt, PEER_TOPK, LANES), lambda gi, ti, hi: (gi, hi, ti, 0, 0))],
        out_shape=[tab, tab, tab, jax.ShapeDtypeStruct(tab.shape, BF16),
                   jax.ShapeDtypeStruct((g, hp, r // LANES, 1, LANES), F32),
                   jax.ShapeDtypeStruct((g, hp, r // LANES, PEER_TOPK, LANES), F32)],
        scratch_shapes=[pltpu.VMEM((2, nk, tr), F32)],
        compiler_params=_params(3),
        name="peer_route",
    )(h, w_q, keys)


_ERF_NUM = (-2.72614225801306e-10, 2.77068142495902e-08, -2.10102402082508e-06, -5.69250639462346e-05,
            -7.34990630326855e-04, -2.95459980854025e-03, -1.60960333262415e-02)
_ERF_DEN = (-1.45660718464996e-05, -2.13374055278905e-04, -1.68282697438203e-03, -7.37332916720468e-03,
            -1.42647390514189e-02)


_GELU_NUM = tuple(c * 0.5 ** (len(_ERF_NUM) - 1 - m) * 0.5 * 2.0 ** -0.5 for m, c in enumerate(_ERF_NUM))
_GELU_DEN = tuple(c * 0.5 ** (len(_ERF_DEN) - 1 - m) for m, c in enumerate(_ERF_DEN))
_GELU_CLAMP = 4.0 * 2.0 ** 0.5


def _gelu(a):
    ac = jnp.clip(a, -_GELU_CLAMP, _GELU_CLAMP)
    a2 = ac * ac
    p = _GELU_NUM[0] * a2 + _GELU_NUM[1]
    for c in _GELU_NUM[2:]:
        p = p * a2 + c
    q = _GELU_DEN[0] * a2 + _GELU_DEN[1]
    for c in _GELU_DEN[2:]:
        q = q * a2 + c
    return a * (0.5 + ac * p / q)


def _peer_dense_kernel(h_ref, u_ref, v_ref, s1_ref, s2_ref, e1_ref, e2_ref, thr_ref, bl_ref, x_ref, gate_ref,
                       o_ref, a_sc, w_sc, *, n_heads, nk, tt, eb):
    e = pl.program_id(2)

    @pl.when(e == 0)
    def _():
        o_ref[0] = jnp.zeros(o_ref.shape[1:], F32)

    a_sc[...] = lax.dot_general(u_ref[...], h_ref[0], _NT, preferred_element_type=F32)

    rows_per = 16
    tile = (rows_per, LANES)
    for lt in range(tt // LANES):
        lanes = slice(lt * LANES, (lt + 1) * LANES)
        t_rows = []
        for h in range(n_heads):
            s1_blk = s1_ref[0, h, lt]
            thr_row = thr_ref[0, h, lt]
            t = jnp.full(s1_blk.shape, jnp.inf, F32)
            for r in range(bl_ref.shape[3]):
                bv = bl_ref[0, h, lt, r:r + 1, :]
                t = jnp.where(s1_blk + bv >= thr_row, bv, t)
            t_rows.append(t)
        for ii in range(eb // nk):
            t_b = [jnp.broadcast_to(t_rows[h][ii:ii + 1, :], tile) for h in range(n_heads)]
            e1_b = [jnp.broadcast_to(e1_ref[0, h, lt, ii:ii + 1, :], tile).astype(BF16) for h in range(n_heads)]
            for jc in range(nk // rows_per):
                jrows = slice(jc * rows_per, (jc + 1) * rows_per)
                rows = slice(ii * nk + jc * rows_per, ii * nk + (jc + 1) * rows_per)
                gsum = jnp.zeros(tile, BF16)
                for h in range(n_heads):
                    hit = s2_ref[0, h, lt, jrows, :] >= t_b[h]
                    gsum = gsum + jnp.where(hit, e1_b[h] * e2_ref[0, h, lt, jrows, :], jnp.zeros(tile, BF16))
                w_sc[rows, lanes] = (gsum.astype(F32) * _gelu(a_sc[rows, lanes])).astype(BF16)

    o_ref[0] += lax.dot_general(w_sc[...], v_ref[...], _TN, preferred_element_type=F32)

    @pl.when(e == pl.num_programs(2) - 1)
    def _():
        o_ref[0] = x_ref[0] + gate_ref[0] * o_ref[0]


def _peer_dense(h, u, v, layer, tables, x, mod, gate_chunk):
    g, r, d = h.shape
    s1, s2, e1, e2, thr, bl = tables
    hp, nk = s1.shape[1], s1.shape[3]
    tt = min(r, ROW_TILE)
    eb = EXPERT_TILE
    per = eb // nk
    assert tt % LANES == 0 and per == 8
    nlt = tt // LANES
    full_spec = pl.BlockSpec((1, hp, nlt, nk, LANES), lambda gi, ti, ei: (gi, 0, ti, 0, 0))
    rows_spec = pl.BlockSpec((1, hp, nlt, per, LANES), lambda gi, ti, ei: (gi, 0, ti, ei, 0))
    return pl.pallas_call(
        functools.partial(_peer_dense_kernel, n_heads=hp, nk=nk, tt=tt, eb=eb),
        grid=(g, r // tt, u.shape[1] // eb),
        in_specs=[pl.BlockSpec((1, tt, d), lambda gi, ti, ei: (gi, ti, 0)),
                  pl.BlockSpec((None, eb, d), lambda gi, ti, ei: (layer, ei, 0)),
                  pl.BlockSpec((None, eb, d), lambda gi, ti, ei: (layer, ei, 0)),
                  rows_spec, full_spec, rows_spec, full_spec,
                  pl.BlockSpec((1, hp, nlt, 1, LANES), lambda gi, ti, ei: (gi, 0, ti, 0, 0)),
                  pl.BlockSpec((1, hp, nlt, bl.shape[3], LANES), lambda gi, ti, ei: (gi, 0, ti, 0, 0)),
                  pl.BlockSpec((1, tt, d), lambda gi, ti, ei: (gi, ti, 0)),
                  _mod_spec(mod, tt, gate_chunk, lambda gi, ti, ei: (gi, ti))],
        out_specs=pl.BlockSpec((1, tt, d), lambda gi, ti, ei: (gi, ti, 0)),
        out_shape=jax.ShapeDtypeStruct((g, r, d), F32),
        scratch_shapes=[pltpu.VMEM((eb, tt), F32), pltpu.VMEM((eb, tt), BF16)],
        compiler_params=_params(3),
        name="peer_dense",
    )(h, u, v, s1, s2, e1, e2, thr, bl, x, mod)


def _trunk(x, mods, is_prompt, state_ret, state_conv, norm_w, final_norm_w,
           ret_w_in, ret_gn_w, ret_w_out, conv_w_pw1, conv_b_pw1, conv_w_dw, conv_b_dw,
           conv_ln_w, conv_ln_b, conv_w_pw2, conv_b_pw2, peer_w_q, peer_keys, peer_u, peer_v):
    depth = norm_w.shape[0]
    n_heads, dk, dv = state_ret.shape[2:]
    new_ret, new_conv = None, []
    ret_prompt_states = []
    vec = lambda a: a[:, None, :]
    for i in range(depth):
        mod = mods[i]
        h = _modnorm(x, norm_w[i, 0], mod, 0, 1)
        j = i // 2
        if i % 2 == 0:
            proj = _mm_plain(h, ret_w_in, j)
            if is_prompt:
                og, r = _ret_prompt(proj, ret_gn_w[j], n_heads, dk, dv)
                ret_prompt_states.append(r)
            else:
                og, new_ret = _ret_sample(proj, state_ret, j, new_ret, ret_gn_w[j], n_heads, dk, dv)
            x = _mm_resid(og, ret_w_out, None, j, x, mod, 2)
        else:
            u = _mm_glu(h, conv_w_pw1, vec(conv_b_pw1), j)
            if is_prompt:
                z, nc = _conv_prompt(u, conv_w_dw, conv_b_dw, conv_ln_w, conv_ln_b, j)
            else:
                z, nc = _conv_sample(u, state_conv, conv_w_dw, conv_b_dw, conv_ln_w, conv_ln_b, j)
            new_conv.append(nc)
            x = _mm_resid(z, conv_w_pw2, vec(conv_b_pw2), j, x, mod, 2)
        h = _modnorm(x, norm_w[i, 1], mod, 3, 4)
        tables = _route(h, peer_w_q, peer_keys, i)
        x = _peer_dense(h, peer_u, peer_v, i, tables, x, mod, 5)
    y = _rmsnorm(x, final_norm_w)
    if is_prompt:
        new_ret = jnp.stack(ret_prompt_states)
    return y, new_ret, jnp.stack(new_conv)


def kernel(x_prompt, x_sample, state_ret, state_conv, c_prompt, c_sample, ada_w, ada_b, norm_w, final_norm_w,
           ret_w_in, ret_gn_w, ret_w_out, conv_w_pw1, conv_b_pw1, conv_w_dw, conv_b_dw, conv_ln_w, conv_ln_b,
           conv_w_pw2, conv_b_pw2, peer_w_q, peer_keys, peer_u, peer_v):
    b, l, d = x_prompt.shape
    bs, ls, _ = x_sample.shape
    assert ls == 1, "the sample path handles one new token per sequence"
    depth = ada_w.shape[0]

    n_c = b + bs
    n_pad = -n_c % 8
    c_all = jnp.concatenate([c_prompt, c_sample, jnp.zeros((n_pad, d), c_prompt.dtype)], axis=0)
    mod = _ada(c_all, ada_w, ada_b)
    mods_p = [mod[i, :b].reshape(b, 1, 6 * d) for i in range(depth)]
    mods_s = [mod[i, b:b + bs].reshape(1, bs, 6 * d) for i in range(depth)]

    weights = (norm_w, final_norm_w, ret_w_in, ret_gn_w, ret_w_out, conv_w_pw1, conv_b_pw1, conv_w_dw, conv_b_dw,
               conv_ln_w, conv_ln_b, conv_w_pw2, conv_b_pw2, peer_w_q, peer_keys,
               peer_u.astype(BF16), peer_v.astype(BF16))
    y_p, ret_p, conv_p = _trunk(x_prompt, mods_p, True, state_ret, state_conv, *weights)
    y_s, ret_s, conv_s = _trunk(x_sample.reshape(1, bs, d), mods_s, False, state_ret, state_conv, *weights)
    return (y_p, y_s.reshape(bs, 1, d), ret_p, conv_p, ret_s, conv_s)
```
